```python
import math
import jax, jax.numpy as jnp
from jax import lax
import numpy as np

D_MODEL = 2048
BATCH = 4
SEQ = 2048
DEPTH = 4
DEC_BATCH = 8
DEC_SEQ = 1
PAST_LEN = 16384
PAGE_SIZE = 128

BRANCH_W = D_MODEL // 2
W_CONV = BRANCH_W
CONV_WIDTH = 3
H_REC = 8
REC_DK = BRANCH_W // H_REC
REC_DV = BRANCH_W // H_REC
W_REC_K = H_REC * REC_DK
W_REC = H_REC * REC_DV
H_ATT = 8
ATT_VD = BRANCH_W // H_ATT
ATT_HD = ATT_VD // 2
W_ATT_QK = H_ATT * 2 * ATT_HD
W_ATT = H_ATT * ATT_VD
ROT_DIM = ATT_HD // 4
ROPE_THETA = 500000.0
N_BRANCH = 3
REC_CHUNK = 64
Q_BLOCK = 128
EPS = 1e-6
MASK_VALUE = -1e30
SPLIT_SIZES = (W_CONV, W_CONV, W_CONV, W_CONV,
               W_REC_K, W_REC_K, W_REC, W_REC,
               W_ATT_QK, W_ATT_QK, W_ATT, W_ATT,
               N_BRANCH * D_MODEL)
N_IN = 4 * W_CONV + 2 * W_REC_K + 2 * W_REC + 2 * W_ATT_QK + 2 * W_ATT + N_BRANCH * D_MODEL

kernel_name = "hybrid_conv_hgrn2_diffattn_step"


def _rmsnorm(x, w):
    xf = x.astype(jnp.float32)
    y = xf * lax.rsqrt(jnp.mean(xf * xf, axis=-1, keepdims=True) + EPS)
    return (y * w.astype(jnp.float32)).astype(x.dtype)


def _rope_partial(x, pos):
    half = ROT_DIM // 2
    inv_freq = ROPE_THETA ** (-jnp.arange(half, dtype=jnp.float32) * 2.0 / ROT_DIM)
    ang = pos.astype(jnp.float32)[:, None] * inv_freq[None, :]
    cos = jnp.cos(ang)[:, None, None, :]
    sin = jnp.sin(ang)[:, None, None, :]
    xf = x.astype(jnp.float32)
    x1 = xf[..., :half]
    x2 = xf[..., half:ROT_DIM]
    out = jnp.concatenate([x1 * cos - x2 * sin, x2 * cos + x1 * sin, xf[..., ROT_DIM:]], axis=-1)
    return out.astype(x.dtype)


def _short_conv(u, buf, w):
    T = u.shape[1]
    full = jnp.concatenate([buf.astype(u.dtype), u], axis=1)
    y = w[0] * full[:, 0:T]
    for j in range(1, CONV_WIDTH):
        y = y + w[j] * full[:, j:j + T]
    return y, full[:, T:]


def _hgrn2_chunked(q, k, v, log_f, s0):
    B, T, H, K = q.shape
    V = v.shape[-1]
    C = min(REC_CHUNK, T)
    n = -(-T // C)
    pad = n * C - T

    def blocks(a):
        a = jnp.pad(a.astype(jnp.float32), ((0, 0), (0, pad), (0, 0), (0, 0)))
        return a.reshape(B, n, C, H, a.shape[-1]).transpose(1, 0, 3, 2, 4)

    causal = jnp.tril(jnp.ones((C, C), dtype=bool))[:, :, None]

    def step(s, inp):
        qc, kc, vc, gc = inp
        b = jnp.cumsum(gc, axis=2)
        diff = b[:, :, :, None, :] - b[:, :, None, :, :]
        decay = jnp.where(causal, jnp.exp(jnp.where(causal, diff, 0.0)), 0.0)
        a = jnp.einsum('bhik,bhijk,bhjk->bhij', qc, decay, kc)
        o = jnp.einsum('bhij,bhjv->bhiv', a, vc) + jnp.einsum('bhik,bhkv->bhiv', qc * jnp.exp(b), s)
        b_last = b[:, :, -1:, :]
        s_new = (jnp.exp(b_last[:, :, 0, :])[..., None] * s
                 + jnp.einsum('bhjk,bhjv->bhkv', kc * jnp.exp(b_last - b), vc))
        return s_new, o

    s_fin, o = lax.scan(step, s0.astype(jnp.float32), (blocks(q), blocks(k), blocks(v), blocks(log_f)))
    o = o.transpose(1, 0, 3, 2, 4).reshape(B, n * C, H, V)[:, :T]
    return o.astype(v.dtype), s_fin.astype(s0.dtype)


def _diff_attention(q, k, v, q_pos, k_pos, lam):
    B, T = q.shape[:2]
    QB = min(Q_BLOCK, T)
    nb = -(-T // QB)
    pad = nb * QB - T
    qp = jnp.pad(q, ((0, 0), (0, pad), (0, 0), (0, 0), (0, 0)))
    qp = qp.reshape(B, nb, QB, H_ATT, 2, ATT_HD).swapaxes(0, 1)
    pp = jnp.pad(q_pos, (0, pad), mode='edge').reshape(nb, QB)
    scale = ATT_HD ** -0.5

    def block(args):
        qb, pb = args
        s = jnp.einsum('bqhcd,bkhcd->bhcqk', qb, k, preferred_element_type=jnp.float32) * scale
        mask = k_pos[None, :] <= pb[:, None]
        p = jax.nn.softmax(jnp.where(mask, s, MASK_VALUE), axis=-1)
        w = p[:, :, 0] - lam * p[:, :, 1]
        return jnp.einsum('bhqk,bkhv->bqhv', w.astype(v.dtype), v)

    out = lax.map(block, (qp, pp))
    return out.swapaxes(0, 1).reshape(B, nb * QB, H_ATT, ATT_VD)[:, :T]


def _layer(x, conv_buf, rec_state, past_k, past_v, lb, lam, lam_init,
           norm_w, w_in, conv_w, rec_out_norm, q_norm, k_norm, attn_subln, w_branch, w_out):
    B, T, _ = x.shape
    P = past_k.shape[1]
    q_pos = P + jnp.arange(T, dtype=jnp.int32)
    k_pos = jnp.arange(P + T, dtype=jnp.int32)

    h = _rmsnorm(x, norm_w)
    proj = h @ w_in
    split_idx = [int(s) for s in np.cumsum(SPLIT_SIZES)[:-1]]
    (c_h, c_b, c_c, c_z, r_q, r_f, r_i, r_g, a_q, a_k, a_v, a_z, gates) = jnp.split(proj, split_idx, axis=-1)

    conv_out, new_conv = _short_conv(c_c * c_h, conv_buf, conv_w)
    y_a = c_b * conv_out * jax.nn.silu(c_z)

    q_r = jax.nn.silu(r_q).reshape(B, T, H_REC, REC_DK)
    rf = r_f.astype(jnp.float32)
    f = lb + (1.0 - lb) * jax.nn.sigmoid(rf)
    log_f = jnp.log(f).reshape(B, T, H_REC, REC_DK)
    k_r = ((1.0 - lb) * jax.nn.sigmoid(-rf)).reshape(B, T, H_REC, REC_DK)
    v_r = r_i.reshape(B, T, H_REC, REC_DV)
    o_r, new_rec = _hgrn2_chunked(q_r, k_r, v_r, log_f, rec_state)
    y_b = (_rmsnorm(o_r, rec_out_norm) * jax.nn.silu(r_g.reshape(B, T, H_REC, REC_DV))).reshape(B, T, W_REC)

    q_a = _rope_partial(_rmsnorm(a_q.reshape(B, T, H_ATT, 2, ATT_HD), q_norm), q_pos)
    k_a = _rope_partial(_rmsnorm(a_k.reshape(B, T, H_ATT, 2, ATT_HD), k_norm), q_pos)
    v_a = a_v.reshape(B, T, H_ATT, ATT_VD)
    keys = jnp.concatenate([past_k.astype(k_a.dtype), k_a], axis=1)
    vals = jnp.concatenate([past_v.astype(v_a.dtype), v_a], axis=1)
    o_a = _diff_attention(q_a, keys, vals, q_pos, k_pos, lam)
    o_a = _rmsnorm(o_a, attn_subln) * (1.0 - lam_init)
    y_c = o_a.reshape(B, T, W_ATT) * jax.nn.silu(a_z)

    ys = jnp.stack([y_a, y_b, y_c], axis=2)
    p = jnp.einsum('btnc,ncd->btnd', ys, w_branch)
    g = jax.nn.sigmoid(gates.reshape(B, T, N_BRANCH, D_MODEL))
    m = jnp.sum(g * p, axis=2)
    y = x + m @ w_out
    return y, new_conv, new_rec, k_a, v_a


def setup_inputs(seed: int = 0) -> dict:
    key = jax.random.key(seed)
    ks = jax.random.split(key, 20)
    n_pages = PAST_LEN // PAGE_SIZE
    n_used = DEC_BATCH * n_pages
    n_pool = n_used + max(1, n_used // 4)
    perm = jax.random.permutation(ks[0], n_pool)
    page_table = perm[:n_used].reshape(DEC_BATCH, n_pages).astype(jnp.int32)
    f32 = jnp.float32
    nrm = lambda k, s: jax.random.normal(k, s, dtype=f32)
    return {
        "x_prompt": nrm(ks[1], (BATCH, SEQ, D_MODEL)),
        "x_sample": nrm(ks[2], (DEC_BATCH, DEC_SEQ, D_MODEL)),
        "cache_k": nrm(ks[3], (DEPTH, n_pool, PAGE_SIZE, H_ATT, 2, ATT_HD)),
        "cache_v": nrm(ks[4], (DEPTH, n_pool, PAGE_SIZE, H_ATT, ATT_VD)),
        "state_conv": nrm(ks[5], (DEPTH, DEC_BATCH, CONV_WIDTH - 1, W_CONV)),
        "state_rec": 0.5 * nrm(ks[6], (DEPTH, DEC_BATCH, H_REC, REC_DK, REC_DV)),
        "page_table": page_table,
        "norm_w": 1.0 + 0.02 * nrm(ks[7], (DEPTH, D_MODEL)),
        "w_in": nrm(ks[8], (DEPTH, D_MODEL, N_IN)) * D_MODEL ** -0.5,
        "conv_w": nrm(ks[9], (DEPTH, CONV_WIDTH, W_CONV)) * CONV_WIDTH ** -0.5,
        "rec_lb_logits": 0.1 * nrm(ks[10], (DEPTH, W_REC_K)),
        "rec_out_norm": 1.0 + 0.02 * nrm(ks[11], (DEPTH, REC_DV)),
        "q_norm": 1.0 + 0.02 * nrm(ks[12], (DEPTH, ATT_HD)),
        "k_norm": 1.0 + 0.02 * nrm(ks[13], (DEPTH, ATT_HD)),
        "lambda_qk": 0.1 * nrm(ks[14], (DEPTH, 4, ATT_HD)),
        "attn_subln": 1.0 + 0.02 * nrm(ks[15], (DEPTH, ATT_VD)),
        "w_branch": nrm(ks[16], (DEPTH, N_BRANCH, BRANCH_W, D_MODEL)) * BRANCH_W ** -0.5,
        "w_out": nrm(ks[17], (DEPTH, D_MODEL, D_MODEL)) * D_MODEL ** -0.5,
    }


def reference(x_prompt, x_sample, cache_k, cache_v, state_conv, state_rec, page_table,
              norm_w, w_in, conv_w, rec_lb_logits, rec_out_norm, q_norm, k_norm, lambda_qk,
              attn_subln, w_branch, w_out):
    n_pages = PAST_LEN // PAGE_SIZE
    soft = jax.nn.softmax(rec_lb_logits.astype(jnp.float32), axis=0)
    lb_all = jnp.cumsum(soft, axis=0) - soft[0:1]

    y_p, y_s = x_prompt, x_sample
    kp_l, vp_l, ks_l, vs_l, cp_l, cs_l, rp_l, rs_l = [], [], [], [], [], [], [], []
    for l in range(DEPTH):
        lam_init = 0.8 - 0.6 * math.exp(-0.3 * l)
        lq = lambda_qk[l].astype(jnp.float32)
        lam = jnp.exp(jnp.sum(lq[0] * lq[1])) - jnp.exp(jnp.sum(lq[2] * lq[3])) + lam_init
        wl = (norm_w[l], w_in[l], conv_w[l], rec_out_norm[l], q_norm[l], k_norm[l], attn_subln[l], w_branch[l], w_out[l])

        zero_conv = jnp.zeros((BATCH, CONV_WIDTH - 1, W_CONV), x_prompt.dtype)
        zero_rec = jnp.zeros((BATCH, H_REC, REC_DK, REC_DV), jnp.float32)
        empty_k = jnp.zeros((BATCH, 0, H_ATT, 2, ATT_HD), x_prompt.dtype)
        empty_v = jnp.zeros((BATCH, 0, H_ATT, ATT_VD), x_prompt.dtype)
        y_p, c_p, r_p, k_p, v_p = _layer(y_p, zero_conv, zero_rec, empty_k, empty_v, lb_all[l], lam, lam_init, *wl)

        past_k = jnp.take(cache_k[l], page_table, axis=0).reshape(DEC_BATCH, n_pages * PAGE_SIZE, H_ATT, 2, ATT_HD)
        past_v = jnp.take(cache_v[l], page_table, axis=0).reshape(DEC_BATCH, n_pages * PAGE_SIZE, H_ATT, ATT_VD)
        y_s, c_s, r_s, k_s, v_s = _layer(y_s, state_conv[l], state_rec[l], past_k, past_v, lb_all[l], lam, lam_init, *wl)

        kp_l.append(k_p); vp_l.append(v_p); ks_l.append(k_s); vs_l.append(v_s)
        cp_l.append(c_p); cs_l.append(c_s); rp_l.append(r_p); rs_l.append(r_s)

    new_k_prompt = jnp.stack(kp_l, axis=0)
    new_v_prompt = jnp.stack(vp_l, axis=0)
    new_k_sample = jnp.stack(ks_l, axis=0)
    new_v_sample = jnp.stack(vs_l, axis=0)
    conv_prompt = jnp.stack(cp_l, axis=0)
    conv_sample = jnp.stack(cs_l, axis=0)
    rec_prompt = jnp.stack(rp_l, axis=0)
    rec_sample = jnp.stack(rs_l, axis=0)
    return (y_p, y_s, new_k_prompt, new_v_prompt, new_k_sample, new_v_sample, conv_prompt, conv_sample, rec_prompt, rec_sample)
```

```python
import functools
import math

import jax
import jax.numpy as jnp
import numpy as np
from jax import lax
from jax.experimental import pallas as pl
from jax.experimental.pallas import tpu as pltpu

D_MODEL = 2048
DEPTH = 4
PAST_LEN = 16384
PAGE_SIZE = 128
BRANCH_W = D_MODEL // 2
CONV_WIDTH = 3
N_HEADS = 8
HEAD_W = BRANCH_W // N_HEADS
ATT_HD = HEAD_W // 2
ROT_DIM = ATT_HD // 4
ROPE_THETA = 500000.0
N_BRANCH = 3
EPS = 1e-6
MASK_VALUE = -1e30
N_IN = 12 * BRANCH_W + N_BRANCH * D_MODEL
COL_CONV, COL_REC, COL_ATT, COL_GATE = 0, 4, 8, 12

V7X_VMEM_BYTES = 64 * 1024 * 1024
BF16_SUBLANES = 16
VMEM_LIMIT = 56 * 1024 * 1024
REC_CHUNK = 64
REC_SUB = 8
PAGES_PER_STEP = 4

f32 = jnp.float32
bf16 = jnp.bfloat16


def _params(sem, vmem=VMEM_LIMIT):
    return pltpu.CompilerParams(dimension_semantics=sem, vmem_limit_bytes=vmem)


def _sigmoid(x):
    return 1.0 / (1.0 + jnp.exp(-x))


def _silu(x):
    return x * _sigmoid(x)


def _dot(a, b):
    return jnp.dot(a, b, preferred_element_type=f32)


def _dot_nt(a, b):
    return lax.dot_general(a, b, (((1,), (1,)), ((), ())), preferred_element_type=f32)


def _dot_tn(a, b):
    return lax.dot_general(a, b, (((0,), (0,)), ((), ())), preferred_element_type=f32)


def _param_kernel(logit_ref, lq_ref, lb_ref, lam_ref):
    x = logit_ref[...]
    m = jnp.max(x, axis=0, keepdims=True)
    e = jnp.exp(x - m)
    soft = e / jnp.sum(e, axis=0, keepdims=True)
    run = soft[0:1]
    rows = [run - soft[0:1]]
    for l in range(1, DEPTH):
        run = run + soft[l:l + 1]
        rows.append(run - soft[0:1])
    lb_ref[...] = jnp.concatenate(rows, axis=0)
    lams = []
    for l in range(DEPTH):
        lq = lq_ref[l]
        a = jnp.sum(lq[0:1] * lq[1:2], axis=-1, keepdims=True)
        b = jnp.sum(lq[2:3] * lq[3:4], axis=-1, keepdims=True)
        lam_init = 0.8 - 0.6 * math.exp(-0.3 * l)
        lams.append(jnp.broadcast_to(jnp.exp(a) - jnp.exp(b) + lam_init, (1, HEAD_W)))
    lam_ref[...] = jnp.concatenate(lams, axis=0)


def _layer_params(rec_lb_logits, lambda_qk):
    return pl.pallas_call(
        _param_kernel,
        out_shape=(jax.ShapeDtypeStruct(rec_lb_logits.shape, f32),
                   jax.ShapeDtypeStruct((DEPTH, HEAD_W), f32)),
    )(rec_lb_logits.astype(f32), lambda_qk.astype(f32))


def _inproj_kernel(x_ref, nw_ref, w_ref, o_ref, h_ref):
    @pl.when(pl.program_id(1) == 0)
    def _():
        x = x_ref[...]
        ms = jnp.mean(x * x, axis=-1, keepdims=True)
        h_ref[...] = (x * lax.rsqrt(ms + EPS) * nw_ref[...]).astype(h_ref.dtype)

    o_ref[...] = _dot(h_ref[...], w_ref[...])


def _in_proj(x, norm_w, w_bf, tm, tn):
    m, d = x.shape
    n = w_bf.shape[1]
    return pl.pallas_call(
        _inproj_kernel,
        grid=(m // tm, n // tn),
        in_specs=[pl.BlockSpec((tm, d), lambda i, j: (i, 0)),
                  pl.BlockSpec((1, d), lambda i, j: (0, 0)),
                  pl.BlockSpec((d, tn), lambda i, j: (0, j))],
        out_specs=pl.BlockSpec((tm, tn), lambda i, j: (i, j)),
        out_shape=jax.ShapeDtypeStruct((m, n), f32),
        scratch_shapes=[pltpu.VMEM((tm, d), bf16)],
        compiler_params=_params(("parallel", "arbitrary")),
    )(x, norm_w.reshape(1, d), w_bf)


def _merge_kernel(ya_ref, yb_ref, yc_ref, wb_ref, ga_ref, gb_ref, gc_ref, o_ref):
    acc = None
    for n, (y_ref, g_ref) in enumerate(((ya_ref, ga_ref), (yb_ref, gb_ref), (yc_ref, gc_ref))):
        t = _sigmoid(g_ref[...]) * _dot(y_ref[...], wb_ref[n])
        acc = t if acc is None else acc + t
    o_ref[...] = acc.astype(o_ref.dtype)


def _merge(ya, yb, yc, wb_bf, proj, tm, tn):
    m = ya.shape[0]
    gate0 = COL_GATE * BRANCH_W // tn
    per = D_MODEL // tn
    y_spec = pl.BlockSpec((tm, BRANCH_W), lambda i, j: (i, 0))
    g_specs = [pl.BlockSpec((tm, tn), functools.partial(lambda i, j, n: (i, gate0 + n * per + j), n=n))
               for n in range(N_BRANCH)]
    return pl.pallas_call(
        _merge_kernel,
        grid=(m // tm, D_MODEL // tn),
        in_specs=[y_spec, y_spec, y_spec,
                  pl.BlockSpec((N_BRANCH, BRANCH_W, tn), lambda i, j: (0, 0, j))] + g_specs,
        out_specs=pl.BlockSpec((tm, tn), lambda i, j: (i, j)),
        out_shape=jax.ShapeDtypeStruct((m, D_MODEL), bf16),
        compiler_params=_params(("parallel", "parallel")),
    )(ya, yb, yc, wb_bf, proj, proj, proj)


def _outproj_kernel(m_ref, w_ref, x_ref, o_ref):
    o_ref[...] = x_ref[...] + _dot(m_ref[...], w_ref[...])


def _out_proj(mm, w_bf, x, tm, tn):
    m = x.shape[0]
    return pl.pallas_call(
        _outproj_kernel,
        grid=(m // tm, D_MODEL // tn),
        in_specs=[pl.BlockSpec((tm, D_MODEL), lambda i, j: (i, 0)),
                  pl.BlockSpec((D_MODEL, tn), lambda i, j: (0, j)),
                  pl.BlockSpec((tm, tn), lambda i, j: (i, j))],
        out_specs=pl.BlockSpec((tm, tn), lambda i, j: (i, j)),
        out_shape=jax.ShapeDtypeStruct((m, D_MODEL), f32),
        compiler_params=_params(("parallel", "parallel")),
    )(mm, w_bf, x)


def _conv_kernel(ch_ref, cb_ref, cc_ref, cz_ref, w_ref, ya_ref, nc_ref, carry_ref, *, tt):
    @pl.when(pl.program_id(1) == 0)
    def _():
        carry_ref[...] = jnp.zeros_like(carry_ref)

    u = cc_ref[0] * ch_ref[0]
    prev2 = carry_ref[0:1, :]
    prev1 = carry_ref[1:2, :]
    row = lax.broadcasted_iota(jnp.int32, u.shape, 0)
    u1 = jnp.where(row == 0, prev1, pltpu.roll(u, 1, axis=0))
    u2 = jnp.where(row == 0, prev2, jnp.where(row == 1, prev1, pltpu.roll(u, 2, axis=0)))
    y = w_ref[0:1, :] * u2 + w_ref[1:2, :] * u1 + w_ref[2:3, :] * u
    ya_ref[0] = (cb_ref[0] * y * _silu(cz_ref[0])).astype(ya_ref.dtype)
    tail = u[tt - 2:tt, :]
    carry_ref[0:2, :] = tail
    nc_ref[0] = tail


def _conv_prompt(proj, conv_w, tt):
    b, t, _ = proj.shape
    specs = [pl.BlockSpec((1, tt, BRANCH_W), functools.partial(lambda i, j, c: (i, j, c), c=COL_CONV + c))
             for c in range(4)]
    return pl.pallas_call(
        functools.partial(_conv_kernel, tt=tt),
        grid=(b, t // tt),
        in_specs=specs + [pl.BlockSpec((CONV_WIDTH, BRANCH_W), lambda i, j: (0, 0))],
        out_specs=(pl.BlockSpec((1, tt, BRANCH_W), lambda i, j: (i, j, 0)),
                   pl.BlockSpec((1, CONV_WIDTH - 1, BRANCH_W), lambda i, j: (i, 0, 0))),
        out_shape=(jax.ShapeDtypeStruct((b, t, BRANCH_W), bf16),
                   jax.ShapeDtypeStruct((b, CONV_WIDTH - 1, BRANCH_W), f32)),
        scratch_shapes=[pltpu.VMEM((8, BRANCH_W), f32)],
        compiler_params=_params(("parallel", "arbitrary")),
    )(proj, proj, proj, proj, conv_w)


def _rec_gates(rq, rf, lb):
    e = jnp.exp(-jnp.abs(rf))
    r = 1.0 / (1.0 + e)
    er = e * r
    pos = rf >= 0
    sig = jnp.where(pos, r, er)
    nsig = jnp.where(pos, er, r)
    g = jnp.log(lb + (1.0 - lb) * sig)
    k = (1.0 - lb) * nsig
    return _silu(rq), k, g


def _hgrn_kernel(rq_ref, rf_ref, ri_ref, rg_ref, lb_ref, onw_ref, yb_ref, st_ref, st_scr, *, tt):
    c_len = REC_CHUNK

    @pl.when(pl.program_id(2) == 0)
    def _():
        st_scr[...] = jnp.zeros_like(st_scr)

    lb = lb_ref[...]
    onw = onw_ref[...]
    ri = lax.broadcasted_iota(jnp.int32, (c_len, c_len), 0)
    ci = lax.broadcasted_iota(jnp.int32, (c_len, c_len), 1)
    tri = (ci <= ri).astype(bf16)
    ones_kc = jnp.ones((HEAD_W, c_len), bf16)
    rowk = lax.broadcasted_iota(jnp.int32, (c_len, HEAD_W), 0)
    level_masks = []
    s = REC_SUB
    while s < c_len:
        shift = int(math.log2(2 * s))
        level_masks.append((s, ((((ri ^ ci) >> shift) | ((ri & s) ^ s) | (ci & s)) == 0)))
        s *= 2
    sub_shift = int(math.log2(REC_SUB))
    diag_masks = [((((ri - ci) ^ d) | ((ri >> sub_shift) ^ (ci >> sub_shift))) == 0) for d in range(REC_SUB)]
    valid_rows = [(rowk & (REC_SUB - 1)) >= d for d in range(REC_SUB)]

    def chunk(c, carry):
        r0 = pl.multiple_of(c * c_len, c_len)
        q, k, g = _rec_gates(rq_ref[0, pl.ds(r0, c_len), :], rf_ref[0, pl.ds(r0, c_len), :], lb)
        v = ri_ref[0, pl.ds(r0, c_len), :]
        v_bf = v.astype(bf16)
        g_hi = g.astype(bf16)
        rem = g - g_hi.astype(f32)
        g_mid = rem.astype(bf16)
        g_lo = (rem - g_mid.astype(f32)).astype(bf16)
        b = _dot(tri, g_lo) + _dot(tri, g_mid) + _dot(tri, g_hi)
        b_last = b[c_len - 1:c_len, :]

        a = jnp.zeros((c_len, c_len), f32)
        for s, mask in level_masks:
            refs = [jnp.broadcast_to(b[m + s - 1:m + s, :], (2 * s, HEAD_W)) for m in range(0, c_len, 2 * s)]
            z = jnp.exp(-jnp.abs(b - jnp.concatenate(refs, axis=0)))
            a = a + jnp.where(mask, _dot_nt((q * z).astype(bf16), (k * z).astype(bf16)), 0.0)
        for d in range(REC_SUB):
            if d == 0:
                p = q * k
            else:
                ex = jnp.where(valid_rows[d], b - pltpu.roll(b, d, axis=0), 0.0)
                p = q * pltpu.roll(k, d, axis=0) * jnp.exp(ex)
            a_d = _dot(p.astype(bf16), ones_kc)
            a = a + jnp.where(diag_masks[d], a_d, 0.0)

        st = st_scr[...]
        o = _dot(a.astype(bf16), v_bf) + _dot_nt((q * jnp.exp(b)).astype(bf16), st.astype(bf16))
        ms = jnp.mean(o * o, axis=-1, keepdims=True)
        y = o * lax.rsqrt(ms + EPS) * onw * _silu(rg_ref[0, pl.ds(r0, c_len), :])
        yb_ref[0, pl.ds(r0, c_len), :] = y.astype(yb_ref.dtype)
        k_dec = (k * jnp.exp(b_last - b)).astype(bf16)
        st_scr[...] = st * jnp.exp(b_last) + _dot_tn(v_bf, k_dec)
        return carry

    lax.fori_loop(0, tt // c_len, chunk, 0)

    @pl.when(pl.program_id(2) == pl.num_programs(2) - 1)
    def _():
        st_ref[0, 0] = st_scr[...]


def _hgrn_prompt(proj, lb, out_norm, tt):
    b, t, _ = proj.shape
    base = COL_REC * N_HEADS
    specs = [pl.BlockSpec((1, tt, HEAD_W),
                          functools.partial(lambda i, h, j, c: (i, j, c + h), c=base + c * N_HEADS))
             for c in range(4)]
    return pl.pallas_call(
        functools.partial(_hgrn_kernel, tt=tt),
        grid=(b, N_HEADS, t // tt),
        in_specs=specs + [pl.BlockSpec((1, HEAD_W), lambda i, h, j: (0, h)),
                          pl.BlockSpec((1, HEAD_W), lambda i, h, j: (0, 0))],
        out_specs=(pl.BlockSpec((1, tt, HEAD_W), lambda i, h, j: (i, j, h)),
                   pl.BlockSpec((1, 1, HEAD_W, HEAD_W), lambda i, h, j: (i, h, 0, 0))),
        out_shape=(jax.ShapeDtypeStruct((b, t, BRANCH_W), bf16),
                   jax.ShapeDtypeStruct((b, N_HEADS, HEAD_W, HEAD_W), f32)),
        scratch_shapes=[pltpu.VMEM((HEAD_W, HEAD_W), f32)],
        compiler_params=_params(("parallel", "parallel", "arbitrary")),
    )(proj, proj, proj, proj, lb.reshape(1, BRANCH_W), out_norm.reshape(1, HEAD_W))


def _rope_tables(pos):
    half = ROT_DIM // 2
    inv_freq = ROPE_THETA ** (-jnp.arange(half, dtype=f32) * 2.0 / ROT_DIM)
    ang = pos.astype(f32)[:, None] * inv_freq[None, :]
    cos, sin = jnp.cos(ang), jnp.sin(ang)
    t = pos.shape[0]
    one = jnp.ones((t, ATT_HD - ROT_DIM), f32)
    zero = jnp.zeros((t, ATT_HD - ROT_DIM), f32)
    zh = jnp.zeros((t, half), f32)
    cos_t = jnp.concatenate([cos, cos, one], axis=-1)
    sa = jnp.concatenate([-sin, zh, zero], axis=-1)
    sb = jnp.concatenate([zh, sin, zero], axis=-1)
    return tuple(jnp.concatenate([x, x], axis=-1) for x in (cos_t, sa, sb))


def _component_mean_sq(x, grp):
    sq = x * x
    hi = sq.astype(bf16)
    lo = (sq - hi.astype(f32)).astype(bf16)
    return (_dot(lo, grp) + _dot(hi, grp)) * (1.0 / ATT_HD)


def _qk_norm_rope(x, w, grp, cos_t, sa, sb):
    xn = x * lax.rsqrt(_component_mean_sq(x, grp) + EPS) * w
    half = ROT_DIM // 2
    return xn * cos_t + pltpu.roll(xn, HEAD_W - half, axis=1) * sa + pltpu.roll(xn, half, axis=1) * sb


def _group_matrix():
    r = lax.broadcasted_iota(jnp.int32, (HEAD_W, HEAD_W), 0)
    c = lax.broadcasted_iota(jnp.int32, (HEAD_W, HEAD_W), 1)
    return ((r < ATT_HD) == (c < ATT_HD)).astype(bf16)


def _qkprep_kernel(aq_ref, ak_ref, av_ref, qn_ref, kn_ref, cos_ref, sa_ref, sb_ref,
                   kf_ref, kb_ref, vb_ref, qz_ref):
    grp = _group_matrix()
    cos_t, sa, sb = cos_ref[...], sa_ref[...], sb_ref[...]
    lane = lax.broadcasted_iota(jnp.int32, cos_t.shape, 1)
    first = lane < ATT_HD
    scale = ATT_HD ** -0.5
    for h in range(N_HEADS):
        sl = slice(h * HEAD_W, (h + 1) * HEAD_W)
        kk = _qk_norm_rope(ak_ref[0, :, sl], kn_ref[...], grp, cos_t, sa, sb)
        kf_ref[0, :, sl] = kk
        kb_ref[0, :, sl] = kk.astype(bf16)
        qq = _qk_norm_rope(aq_ref[0, :, sl], qn_ref[...], grp, cos_t, sa, sb) * scale
        qz_ref[0, 0, :, sl] = jnp.where(first, qq, 0.0).astype(bf16)
        qz_ref[0, 1, :, sl] = jnp.where(first, 0.0, qq).astype(bf16)
    vb_ref[0] = av_ref[0].astype(bf16)


def _qk_prep(proj, q_norm, k_norm, tables, tt):
    b, t, _ = proj.shape
    specs = [pl.BlockSpec((1, tt, BRANCH_W), functools.partial(lambda i, j, c: (i, j, c), c=COL_ATT + c))
             for c in range(3)]
    w_spec = pl.BlockSpec((1, HEAD_W), lambda i, j: (0, 0))
    t_spec = pl.BlockSpec((tt, HEAD_W), lambda i, j: (j, 0))
    o_spec = pl.BlockSpec((1, tt, BRANCH_W), lambda i, j: (i, j, 0))
    tile2 = lambda w: jnp.concatenate([w, w]).reshape(1, HEAD_W)
    return pl.pallas_call(
        _qkprep_kernel,
        grid=(b, t // tt),
        in_specs=specs + [w_spec, w_spec, t_spec, t_spec, t_spec],
        out_specs=(o_spec, o_spec, o_spec,
                   pl.BlockSpec((1, 2, tt, BRANCH_W), lambda i, j: (i, 0, j, 0))),
        out_shape=(jax.ShapeDtypeStruct((b, t, BRANCH_W), f32),
                   jax.ShapeDtypeStruct((b, t, BRANCH_W), bf16),
                   jax.ShapeDtypeStruct((b, t, BRANCH_W), bf16),
                   jax.ShapeDtypeStruct((b, 2, t, BRANCH_W), bf16)),
        compiler_params=_params(("parallel", "parallel")),
    )(proj, proj, proj, tile2(q_norm), tile2(k_norm), *tables)


def _subln_gate(o, subln, coef, az):
    ms = jnp.mean(o * o, axis=-1, keepdims=True)
    return o * lax.rsqrt(ms + EPS) * subln * coef * _silu(az)


def _flash_kernel(qz_ref, k_ref, v_ref, az_ref, lam_ref, sub_ref, o_ref, m_scr, l_scr, acc_scr,
                  *, tq, coef):
    qi = pl.program_id(2)
    ki = pl.program_id(3)

    @pl.when(ki == 0)
    def _():
        m_scr[...] = jnp.full_like(m_scr, MASK_VALUE)
        l_scr[...] = jnp.zeros_like(l_scr)
        acc_scr[...] = jnp.zeros_like(acc_scr)

    @pl.when(ki <= qi)
    def _():
        q = qz_ref[0].reshape(2 * tq, HEAD_W)
        s = _dot_nt(q, k_ref[0])
        r = lax.broadcasted_iota(jnp.int32, s.shape, 0)
        c = lax.broadcasted_iota(jnp.int32, s.shape, 1)
        qpos = qi * tq + jnp.where(r >= tq, r - tq, r)
        s = jnp.where(ki * tq + c <= qpos, s, MASK_VALUE)
        m_old = m_scr[...]
        m_new = jnp.maximum(m_old, jnp.max(s, axis=-1, keepdims=True))
        alpha = jnp.exp(m_old - m_new)
        p = jnp.exp(s - m_new[:, 0:1])
        l_scr[...] = alpha * l_scr[...] + jnp.sum(p, axis=-1, keepdims=True)
        acc_scr[...] = alpha * acc_scr[...] + _dot(p.astype(bf16), v_ref[0])
        m_scr[...] = m_new

    @pl.when(ki == qi)
    def _():
        on = acc_scr[...] / l_scr[...]
        o = on[0:tq] - lam_ref[...] * on[tq:2 * tq]
        o_ref[0] = _subln_gate(o, sub_ref[...], coef, az_ref[0]).astype(o_ref.dtype)


def _flash_prompt(qz, k_bf, v_bf, proj, lam_row, subln, coef, tq):
    b, t, _ = k_bf.shape
    n = t // tq
    az_col = (COL_ATT + 3) * N_HEADS
    return pl.pallas_call(
        functools.partial(_flash_kernel, tq=tq, coef=coef),
        grid=(b, N_HEADS, n, n),
        in_specs=[pl.BlockSpec((1, 2, tq, HEAD_W), lambda i, h, q, k: (i, 0, q, h)),
                  pl.BlockSpec((1, tq, HEAD_W), lambda i, h, q, k: (i, jnp.minimum(k, q), h)),
                  pl.BlockSpec((1, tq, HEAD_W), lambda i, h, q, k: (i, jnp.minimum(k, q), h)),
                  pl.BlockSpec((1, tq, HEAD_W), lambda i, h, q, k: (i, q, az_col + h)),
                  pl.BlockSpec((1, HEAD_W), lambda i, h, q, k: (0, 0)),
                  pl.BlockSpec((1, HEAD_W), lambda i, h, q, k: (0, 0))],
        out_specs=pl.BlockSpec((1, tq, HEAD_W), lambda i, h, q, k: (i, q, h)),
        out_shape=jax.ShapeDtypeStruct((b, t, BRANCH_W), bf16),
        scratch_shapes=[pltpu.VMEM((2 * tq, HEAD_W), f32),
                        pltpu.VMEM((2 * tq, HEAD_W), f32),
                        pltpu.VMEM((2 * tq, HEAD_W), f32)],
        compiler_params=_params(("parallel", "parallel", "parallel", "arbitrary")),
    )(qz, k_bf, v_bf, proj, lam_row, subln.reshape(1, HEAD_W))


def _lane_to_sublane(row):
    r = lax.broadcasted_iota(jnp.int32, (HEAD_W, HEAD_W), 0)
    c = lax.broadcasted_iota(jnp.int32, (HEAD_W, HEAD_W), 1)
    return jnp.sum(jnp.where(r == c, jnp.broadcast_to(row, (HEAD_W, HEAD_W)), 0.0), axis=1, keepdims=True)


def _sample_mix_kernel(p_ref, cbuf_ref, st_ref, cw_ref, lb_ref, onw_ref, qn_ref, kn_ref,
                       cos_ref, sa_ref, sb_ref,
                       ya_ref, yb_ref, nc_ref, nst_ref, kf_ref, qm_ref):
    blk = lambda c: p_ref[0, c * N_HEADS:(c + 1) * N_HEADS, :]
    u = blk(COL_CONV + 2) * blk(COL_CONV)
    y = cw_ref[0] * cbuf_ref[0, 0] + cw_ref[1] * cbuf_ref[0, 1] + cw_ref[2] * u
    ya_ref[0] = (blk(COL_CONV + 1) * y * _silu(blk(COL_CONV + 3))).astype(ya_ref.dtype)
    nc_ref[0, 0] = cbuf_ref[0, 1]
    nc_ref[0, 1] = u
    q, k, g = _rec_gates(blk(COL_REC), blk(COL_REC + 1), lb_ref[...])
    v = blk(COL_REC + 2)
    dec = jnp.exp(g)
    outs = []
    for h in range(N_HEADS):
        st_new = st_ref[0, h] * dec[h:h + 1, :] + _lane_to_sublane(v[h:h + 1, :]) * k[h:h + 1, :]
        nst_ref[0, h] = st_new
        qh = jnp.broadcast_to(q[h:h + 1, :], (8, HEAD_W)).astype(bf16)
        outs.append(_dot_nt(qh, st_new.astype(bf16))[0:1, :])
    o = jnp.concatenate(outs, axis=0)
    ms = jnp.mean(o * o, axis=-1, keepdims=True)
    yb_ref[0] = (o * lax.rsqrt(ms + EPS) * onw_ref[...] * _silu(blk(COL_REC + 3))).astype(yb_ref.dtype)
    grp = _group_matrix()
    cos_t, sa, sb = cos_ref[...], sa_ref[...], sb_ref[...]
    kf_ref[0] = _qk_norm_rope(blk(COL_ATT + 1), kn_ref[...], grp, cos_t, sa, sb)
    qq = _qk_norm_rope(blk(COL_ATT), qn_ref[...], grp, cos_t, sa, sb) * (ATT_HD ** -0.5)
    r = lax.broadcasted_iota(jnp.int32, (2 * N_HEADS, HEAD_W), 0)
    lane = lax.broadcasted_iota(jnp.int32, (2 * N_HEADS, HEAD_W), 1)
    for h in range(N_HEADS):
        qh = jnp.broadcast_to(qq[h:h + 1, :], (2 * N_HEADS, HEAD_W))
        keep = r == 2 * h + jnp.where(lane < ATT_HD, 0, 1)
        qm_ref[0, :, h * HEAD_W:(h + 1) * HEAD_W] = jnp.where(keep, qh, 0.0).astype(bf16)


def _sample_mix(proj_s, conv_state, st_t, conv_w, lb, out_norm, q_norm, k_norm, tables):
    nb = proj_s.shape[0]
    rows = N_IN // HEAD_W
    tile2 = lambda w: jnp.concatenate([w, w]).reshape(1, HEAD_W)
    vec = pl.BlockSpec((1, HEAD_W), lambda i: (0, 0))
    hw = (N_HEADS, HEAD_W)
    out_row = lambda dt: (pl.BlockSpec((1,) + hw, lambda i: (i, 0, 0)), jax.ShapeDtypeStruct((nb,) + hw, dt))
    outs = [out_row(f32), out_row(f32),
            (pl.BlockSpec((1, CONV_WIDTH - 1) + hw, lambda i: (i, 0, 0, 0)),
             jax.ShapeDtypeStruct((nb, CONV_WIDTH - 1) + hw, f32)),
            (pl.BlockSpec((1, N_HEADS, HEAD_W, HEAD_W), lambda i: (i, 0, 0, 0)),
             jax.ShapeDtypeStruct((nb, N_HEADS, HEAD_W, HEAD_W), f32)),
            out_row(f32),
            (pl.BlockSpec((1, 2 * N_HEADS, BRANCH_W), lambda i: (i, 0, 0)),
             jax.ShapeDtypeStruct((nb, 2 * N_HEADS, BRANCH_W), bf16))]
    return pl.pallas_call(
        _sample_mix_kernel,
        grid=(nb,),
        in_specs=[pl.BlockSpec((1, rows, HEAD_W), lambda i: (i, 0, 0)),
                  pl.BlockSpec((1, CONV_WIDTH - 1) + hw, lambda i: (i, 0, 0, 0)),
                  pl.BlockSpec((1, N_HEADS, HEAD_W, HEAD_W), lambda i: (i, 0, 0, 0)),
                  pl.BlockSpec((CONV_WIDTH,) + hw, lambda i: (0, 0, 0)),
                  pl.BlockSpec(hw, lambda i: (0, 0)),
                  vec, vec, vec, vec, vec, vec],
        out_specs=tuple(o[0] for o in outs),
        out_shape=tuple(o[1] for o in outs),
        compiler_params=_params(("parallel",)),
    )(proj_s.reshape(nb, rows, HEAD_W), conv_state.reshape((nb, CONV_WIDTH - 1) + hw), st_t,
      conv_w.reshape((CONV_WIDTH,) + hw), lb.reshape(hw), out_norm.reshape(1, HEAD_W),
      tile2(q_norm), tile2(k_norm), *tables)


def _decode_kernel(pt_ref, qm_ref, kn_ref, vn_ref, az_ref, lam_ref, sub_ref, *rest, coef):
    g = PAGES_PER_STEP
    k_refs, v_refs = rest[:g], rest[g:2 * g]
    o_ref, m_scr, l_scr, acc_scr = rest[2 * g:]
    step = pl.program_id(1)
    q = qm_ref[0]

    @pl.when(step == 0)
    def _():
        s0 = jnp.sum(q.astype(f32) * kn_ref[0], axis=-1, keepdims=True)
        m_scr[...] = jnp.broadcast_to(s0, m_scr.shape)
        l_scr[...] = jnp.ones_like(l_scr)
        acc_scr[...] = jnp.broadcast_to(vn_ref[0], acc_scr.shape)

    s = jnp.concatenate([_dot_nt(q, k_refs[i][0, 0].astype(bf16)) for i in range(g)], axis=-1)
    m_old = m_scr[...]
    m_new = jnp.maximum(m_old, jnp.max(s, axis=-1, keepdims=True))
    alpha = jnp.exp(m_old - m_new)
    p = jnp.exp(s - m_new[:, 0:1])
    l_scr[...] = alpha * l_scr[...] + jnp.sum(p, axis=-1, keepdims=True)
    pv = _dot(p[:, 0:PAGE_SIZE].astype(bf16), v_refs[0][0, 0].astype(bf16))
    for i in range(1, g):
        pv = pv + _dot(p[:, i * PAGE_SIZE:(i + 1) * PAGE_SIZE].astype(bf16), v_refs[i][0, 0].astype(bf16))
    acc_scr[...] = alpha[:, 0:1] * acc_scr[...] + pv
    m_scr[...] = m_new

    @pl.when(step == pl.num_programs(1) - 1)
    def _():
        r = lax.broadcasted_iota(jnp.int32, acc_scr.shape, 0)
        lane = lax.broadcasted_iota(jnp.int32, acc_scr.shape, 1)
        lam_b = jnp.concatenate([lam_ref[...]] * N_HEADS, axis=-1)
        sign = jnp.where((r & 1) == 0, 1.0, -lam_b)
        own_head = (lane >> int(math.log2(HEAD_W))) == (r >> 1)
        on = jnp.where(own_head, acc_scr[...] / l_scr[:, 0:1] * sign, 0.0)
        o = jnp.sum(on, axis=0, keepdims=True)
        outs = [_subln_gate(o[:, h * HEAD_W:(h + 1) * HEAD_W], sub_ref[...], coef,
                            az_ref[0][:, h * HEAD_W:(h + 1) * HEAD_W]) for h in range(N_HEADS)]
        o_ref[0] = jnp.concatenate(outs, axis=-1).astype(o_ref.dtype)


def _decode_attention(page_table, qmat, k_new, v_new, az, lam_row, subln, cache_k, cache_v, layer, coef):
    nb, n_pages = page_table.shape
    g = PAGES_PER_STEP
    row = pl.BlockSpec((1, 1, BRANCH_W), lambda b, s, pt: (b, 0, 0))
    vec = pl.BlockSpec((1, HEAD_W), lambda b, s, pt: (0, 0))
    page = [pl.BlockSpec((1, 1, PAGE_SIZE, BRANCH_W),
                         functools.partial(lambda b, s, pt, i: (layer, pt[b, s * g + i], 0, 0), i=i))
            for i in range(g)]
    grid_spec = pltpu.PrefetchScalarGridSpec(
        num_scalar_prefetch=1,
        grid=(nb, n_pages // g),
        in_specs=[pl.BlockSpec((1, 2 * N_HEADS, BRANCH_W), lambda b, s, pt: (b, 0, 0)),
                  row, row, row, vec, vec] + page + page,
        out_specs=row,
        scratch_shapes=[pltpu.VMEM((2 * N_HEADS, HEAD_W), f32),
                        pltpu.VMEM((2 * N_HEADS, HEAD_W), f32),
                        pltpu.VMEM((2 * N_HEADS, BRANCH_W), f32)])
    return pl.pallas_call(
        functools.partial(_decode_kernel, coef=coef),
        grid_spec=grid_spec,
        out_shape=jax.ShapeDtypeStruct((nb, 1, BRANCH_W), f32),
        compiler_params=_params(("parallel", "arbitrary")),
    )(page_table, qmat, k_new, v_new, az, lam_row, subln.reshape(1, HEAD_W),
      *([cache_k] * g), *([cache_v] * g))


def kernel(x_prompt, x_sample, cache_k, cache_v, state_conv, state_rec, page_table, norm_w, w_in, conv_w,
           rec_lb_logits, rec_out_norm, q_norm, k_norm, lambda_qk, attn_subln, w_branch, w_out):
    nbp, seq, _ = x_prompt.shape
    nbs = x_sample.shape[0]
    n_pool = cache_k.shape[1]
    lb_all, lam_all = _layer_params(rec_lb_logits, lambda_qk)
    w_in_bf, w_branch_bf, w_out_bf = w_in.astype(bf16), w_branch.astype(bf16), w_out.astype(bf16)
    ck = cache_k.reshape(DEPTH, n_pool, PAGE_SIZE, BRANCH_W)
    cv = cache_v.reshape(DEPTH, n_pool, PAGE_SIZE, BRANCH_W)
    tab_p = _rope_tables(jnp.arange(seq, dtype=jnp.int32))
    tab_s = _rope_tables(jnp.full((1,), page_table.shape[1] * PAGE_SIZE, jnp.int32))

    mp = nbp * seq
    tm = min(1024, mp)
    tt = min(512, seq)
    y_p = x_prompt.reshape(mp, D_MODEL)
    ms = -(-nbs // BF16_SUBLANES) * BF16_SUBLANES
    pad_rows = lambda a: jnp.pad(a, ((0, ms - nbs), (0, 0)))
    y_s = pad_rows(x_sample.reshape(nbs, D_MODEL))
    outs = [[] for _ in range(8)]
    for l in range(DEPTH):
        coef = 1.0 - (0.8 - 0.6 * math.exp(-0.3 * l))
        lam_row = lam_all[l:l + 1]

        proj = _in_proj(y_p, norm_w[l], w_in_bf[l], tm, 1024)
        proj3 = proj.reshape(nbp, seq, N_IN)
        ya, c_p = _conv_prompt(proj3, conv_w[l], tt)
        yb, r_p = _hgrn_prompt(proj3, lb_all[l], rec_out_norm[l], tt)
        k_p, k_bf, v_bf, qz = _qk_prep(proj3, q_norm[l], k_norm[l], tab_p, min(256, seq))
        yc = _flash_prompt(qz, k_bf, v_bf, proj3, lam_row, attn_subln[l], coef, tt)
        mm = _merge(ya.reshape(mp, BRANCH_W), yb.reshape(mp, BRANCH_W), yc.reshape(mp, BRANCH_W),
                    w_branch_bf[l], proj, tm, 512)
        y_p = _out_proj(mm, w_out_bf[l], y_p, tm, 1024)
        v_p = proj3[:, :, (COL_ATT + 2) * BRANCH_W:(COL_ATT + 3) * BRANCH_W]

        proj_s = _in_proj(y_s, norm_w[l], w_in_bf[l], ms, 1024)[:nbs]
        ya_s, yb_s, c_s, st_s, k_s, qmat = _sample_mix(
            proj_s, state_conv[l], jnp.swapaxes(state_rec[l], -1, -2), conv_w[l], lb_all[l],
            rec_out_norm[l], q_norm[l], k_norm[l], tab_s)
        v_s = proj_s[:, (COL_ATT + 2) * BRANCH_W:(COL_ATT + 3) * BRANCH_W]
        az_s = proj_s[:, (COL_ATT + 3) * BRANCH_W:(COL_ATT + 4) * BRANCH_W]
        yc_s = _decode_attention(page_table, qmat, k_s.reshape(nbs, 1, BRANCH_W), v_s.reshape(nbs, 1, BRANCH_W),
                                 az_s.reshape(nbs, 1, BRANCH_W), lam_row, attn_subln[l], ck, cv, l, coef)
        mm_s = _merge(pad_rows(ya_s.reshape(nbs, BRANCH_W)).astype(bf16),
                      pad_rows(yb_s.reshape(nbs, BRANCH_W)).astype(bf16),
                      pad_rows(yc_s.reshape(nbs, BRANCH_W)).astype(bf16),
                      w_branch_bf[l], pad_rows(proj_s), ms, 512)
        y_s = _out_proj(mm_s, w_out_bf[l], y_s, ms, 1024)

        for lst, val in zip(outs, (
                k_p.reshape(nbp, seq, N_HEADS, 2, ATT_HD), v_p.reshape(nbp, seq, N_HEADS, HEAD_W),
                k_s.reshape(nbs, 1, N_HEADS, 2, ATT_HD), v_s.reshape(nbs, 1, N_HEADS, HEAD_W),
                c_p, c_s.reshape(nbs, CONV_WIDTH - 1, BRANCH_W),
                jnp.swapaxes(r_p, -1, -2), jnp.swapaxes(st_s, -1, -2))):
            lst.append(val)

    return (y_p.reshape(nbp, seq, D_MODEL), y_s[:nbs].reshape(nbs, 1, D_MODEL)) + tuple(jnp.stack(o, axis=0) for o in outs)
```

```python
import functools
import math

import jax
import jax.numpy as jnp
import numpy as np
from jax import lax
from jax.experimental import pallas as pl
from jax.experimental.pallas import tpu as pltpu

D_MODEL = 2048
DEPTH = 4
PAST_LEN = 16384
PAGE_SIZE = 128
BRANCH_W = D_MODEL // 2
CONV_WIDTH = 3
N_HEADS = 8
HEAD_W = BRANCH_W // N_HEADS
ATT_HD = HEAD_W // 2
ROT_DIM = ATT_HD // 4
ROPE_THETA = 500000.0
N_BRANCH = 3
EPS = 1e-6
MASK_VALUE = -1e30
N_IN = 12 * BRANCH_W + N_BRANCH * D_MODEL
COL_CONV, COL_REC, COL_ATT, COL_GATE = 0, 4, 8, 12

V7X_VMEM_BYTES = 64 * 1024 * 1024
BF16_SUBLANES = 16
VMEM_LIMIT = 56 * 1024 * 1024
REC_CHUNK = 64
REC_SUB = 8
REC_HEADS_PER_STEP = 4
REC_UNROLL = 1
Q_SCALE = ATT_HD ** -0.5 * math.log2(math.e)
FLASH_ROWS = 256
PAGES_PER_STEP = 4

f32 = jnp.float32
bf16 = jnp.bfloat16


def _params(sem, vmem=VMEM_LIMIT):
    return pltpu.CompilerParams(dimension_semantics=sem, vmem_limit_bytes=vmem)


def _sigmoid(x):
    return 1.0 / (1.0 + jnp.exp(-x))


def _silu(x):
    return x * _sigmoid(x)


def _dot(a, b):
    return jnp.dot(a, b, preferred_element_type=f32)


def _dot_nt(a, b):
    return lax.dot_general(a, b, (((1,), (1,)), ((), ())), preferred_element_type=f32)


def _dot_tn(a, b):
    return lax.dot_general(a, b, (((0,), (0,)), ((), ())), preferred_element_type=f32)


def _param_kernel(logit_ref, lq_ref, lb_ref, lam_ref):
    x = logit_ref[...]
    m = jnp.max(x, axis=0, keepdims=True)
    e = jnp.exp(x - m)
    soft = e / jnp.sum(e, axis=0, keepdims=True)
    run = soft[0:1]
    rows = [run - soft[0:1]]
    for l in range(1, DEPTH):
        run = run + soft[l:l + 1]
        rows.append(run - soft[0:1])
    lb_ref[...] = jnp.concatenate(rows, axis=0)
    lams = []
    for l in range(DEPTH):
        lq = lq_ref[l]
        a = jnp.sum(lq[0:1] * lq[1:2], axis=-1, keepdims=True)
        b = jnp.sum(lq[2:3] * lq[3:4], axis=-1, keepdims=True)
        lam_init = 0.8 - 0.6 * math.exp(-0.3 * l)
        lams.append(jnp.broadcast_to(jnp.exp(a) - jnp.exp(b) + lam_init, (1, HEAD_W)))
    lam_ref[...] = jnp.concatenate(lams, axis=0)


def _layer_params(rec_lb_logits, lambda_qk):
    return pl.pallas_call(
        _param_kernel,
        out_shape=(jax.ShapeDtypeStruct(rec_lb_logits.shape, f32),
                   jax.ShapeDtypeStruct((DEPTH, HEAD_W), f32)),
    )(rec_lb_logits.astype(f32), lambda_qk.astype(f32))


def _inproj_kernel(x_ref, nw_ref, w_ref, o_ref, h_ref):
    @pl.when(pl.program_id(1) == 0)
    def _():
        x = x_ref[...]
        ms = jnp.mean(x * x, axis=-1, keepdims=True)
        h_ref[...] = (x * lax.rsqrt(ms + EPS) * nw_ref[...]).astype(h_ref.dtype)

    o_ref[...] = _dot(h_ref[...], w_ref[...])


def _in_proj(x, norm_w, w_bf, layer, tm, tn):
    m, d = x.shape
    n = w_bf.shape[2]
    return pl.pallas_call(
        _inproj_kernel,
        grid=(m // tm, n // tn),
        in_specs=[pl.BlockSpec((tm, d), lambda i, j: (i, 0)),
                  pl.BlockSpec((1, d), lambda i, j: (0, 0)),
                  pl.BlockSpec((None, d, tn), lambda i, j: (layer, 0, j))],
        out_specs=pl.BlockSpec((tm, tn), lambda i, j: (i, j)),
        out_shape=jax.ShapeDtypeStruct((m, n), f32),
        scratch_shapes=[pltpu.VMEM((tm, d), bf16)],
        compiler_params=_params(("parallel", "arbitrary")),
    )(x, norm_w.reshape(1, d), w_bf)


def _merge_kernel(ya_ref, yb_ref, yc_ref, wb_ref, ga_ref, gb_ref, gc_ref, o_ref):
    acc = None
    for n, (y_ref, g_ref) in enumerate(((ya_ref, ga_ref), (yb_ref, gb_ref), (yc_ref, gc_ref))):
        t = _sigmoid(g_ref[...]) * _dot(y_ref[...], wb_ref[n])
        acc = t if acc is None else acc + t
    o_ref[...] = acc.astype(o_ref.dtype)


def _merge(ya, yb, yc, wb_bf, layer, proj, tm, tn):
    m = ya.shape[0]
    gate0 = COL_GATE * BRANCH_W // tn
    per = D_MODEL // tn
    y_spec = pl.BlockSpec((tm, BRANCH_W), lambda i, j: (i, 0))
    g_specs = [pl.BlockSpec((tm, tn), functools.partial(lambda i, j, n: (i, gate0 + n * per + j), n=n))
               for n in range(N_BRANCH)]
    return pl.pallas_call(
        _merge_kernel,
        grid=(m // tm, D_MODEL // tn),
        in_specs=[y_spec, y_spec, y_spec,
                  pl.BlockSpec((None, N_BRANCH, BRANCH_W, tn), lambda i, j: (layer, 0, 0, j))] + g_specs,
        out_specs=pl.BlockSpec((tm, tn), lambda i, j: (i, j)),
        out_shape=jax.ShapeDtypeStruct((m, D_MODEL), bf16),
        compiler_params=_params(("parallel", "parallel")),
    )(ya, yb, yc, wb_bf, proj, proj, proj)


def _outproj_kernel(m_ref, w_ref, x_ref, o_ref):
    o_ref[...] = x_ref[...] + _dot(m_ref[...], w_ref[...])


def _out_proj(mm, w_bf, layer, x, tm, tn):
    m = x.shape[0]
    return pl.pallas_call(
        _outproj_kernel,
        grid=(m // tm, D_MODEL // tn),
        in_specs=[pl.BlockSpec((tm, D_MODEL), lambda i, j: (i, 0)),
                  pl.BlockSpec((None, D_MODEL, tn), lambda i, j: (layer, 0, j)),
                  pl.BlockSpec((tm, tn), lambda i, j: (i, j))],
        out_specs=pl.BlockSpec((tm, tn), lambda i, j: (i, j)),
        out_shape=jax.ShapeDtypeStruct((m, D_MODEL), f32),
        compiler_params=_params(("parallel", "parallel")),
    )(mm, w_bf, x)


def _conv_kernel(ch_ref, cb_ref, cc_ref, cz_ref, w_ref, ya_ref, nc_ref, carry_ref, *, tt):
    @pl.when(pl.program_id(1) == 0)
    def _():
        carry_ref[...] = jnp.zeros_like(carry_ref)

    u = cc_ref[0] * ch_ref[0]
    prev2 = carry_ref[0:1, :]
    prev1 = carry_ref[1:2, :]
    row = lax.broadcasted_iota(jnp.int32, u.shape, 0)
    u1 = jnp.where(row == 0, prev1, pltpu.roll(u, 1, axis=0))
    u2 = jnp.where(row == 0, prev2, jnp.where(row == 1, prev1, pltpu.roll(u, 2, axis=0)))
    y = w_ref[0:1, :] * u2 + w_ref[1:2, :] * u1 + w_ref[2:3, :] * u
    ya_ref[0] = (cb_ref[0] * y * _silu(cz_ref[0])).astype(ya_ref.dtype)
    tail = u[tt - 2:tt, :]
    carry_ref[0:2, :] = tail
    nc_ref[0] = tail


def _conv_prompt(proj, conv_w, tt):
    b, t, _ = proj.shape
    specs = [pl.BlockSpec((1, tt, BRANCH_W), functools.partial(lambda i, j, c: (i, j, c), c=COL_CONV + c))
             for c in range(4)]
    return pl.pallas_call(
        functools.partial(_conv_kernel, tt=tt),
        grid=(b, t // tt),
        in_specs=specs + [pl.BlockSpec((CONV_WIDTH, BRANCH_W), lambda i, j: (0, 0))],
        out_specs=(pl.BlockSpec((1, tt, BRANCH_W), lambda i, j: (i, j, 0)),
                   pl.BlockSpec((1, CONV_WIDTH - 1, BRANCH_W), lambda i, j: (i, 0, 0))),
        out_shape=(jax.ShapeDtypeStruct((b, t, BRANCH_W), bf16),
                   jax.ShapeDtypeStruct((b, CONV_WIDTH - 1, BRANCH_W), f32)),
        scratch_shapes=[pltpu.VMEM((8, BRANCH_W), f32)],
        compiler_params=_params(("parallel", "arbitrary")),
    )(proj, proj, proj, proj, conv_w)


def _rec_gates(rq, rf, lb):
    e = jnp.exp(-jnp.abs(rf))
    r = 1.0 / (1.0 + e)
    er = e * r
    pos = rf >= 0
    sig = jnp.where(pos, r, er)
    nsig = jnp.where(pos, er, r)
    g = jnp.log(lb + (1.0 - lb) * sig)
    k = (1.0 - lb) * nsig
    return _silu(rq), k, g


def _hgrn_kernel(rq_ref, rf_ref, ri_ref, rg_ref, lb_ref, onw_ref, yb_ref, st_ref, st_scr, *, tt, hp, unroll):
    c_len = REC_CHUNK

    @pl.when(pl.program_id(2) == 0)
    def _():
        st_scr[...] = jnp.zeros_like(st_scr)

    onw = onw_ref[...]
    ri = lax.broadcasted_iota(jnp.int32, (c_len, c_len), 0)
    ci = lax.broadcasted_iota(jnp.int32, (c_len, c_len), 1)
    tri = (ci <= ri).astype(bf16)
    ones_kc = jnp.ones((HEAD_W, c_len), bf16)
    rowk = lax.broadcasted_iota(jnp.int32, (c_len, HEAD_W), 0)
    level_masks = []
    s = REC_SUB
    while s < c_len:
        shift = int(math.log2(2 * s))
        level_masks.append((s, ((((ri ^ ci) >> shift) | ((ri & s) ^ s) | (ci & s)) == 0)))
        s *= 2
    sub_shift = int(math.log2(REC_SUB))
    diag_masks = [((((ri - ci) ^ d) | ((ri >> sub_shift) ^ (ci >> sub_shift))) == 0) for d in range(REC_SUB)]
    valid_rows = [(rowk & (REC_SUB - 1)) >= d for d in range(REC_SUB)]

    def head_chunk(r0, hh):
        rows = pl.ds(r0, c_len)
        lanes = slice(hh * HEAD_W, (hh + 1) * HEAD_W)
        q, k, g = _rec_gates(rq_ref[0, rows, lanes], rf_ref[0, rows, lanes], lb_ref[:, lanes])
        v_bf = ri_ref[0, rows, lanes].astype(bf16)
        g_hi = g.astype(bf16)
        rem = g - g_hi.astype(f32)
        g_mid = rem.astype(bf16)
        g_lo = (rem - g_mid.astype(f32)).astype(bf16)
        b = _dot(tri, g_lo) + _dot(tri, g_mid) + _dot(tri, g_hi)
        b_last = b[c_len - 1:c_len, :]

        a = jnp.zeros((c_len, c_len), f32)
        for s, mask in level_masks:
            refs = [jnp.broadcast_to(b[m + s - 1:m + s, :], (2 * s, HEAD_W)) for m in range(0, c_len, 2 * s)]
            z = jnp.exp(-jnp.abs(b - jnp.concatenate(refs, axis=0)))
            a = a + jnp.where(mask, _dot_nt((q * z).astype(bf16), (k * z).astype(bf16)), 0.0)
        for d in range(REC_SUB):
            if d == 0:
                p = q * k
            else:
                ex = jnp.where(valid_rows[d], b - pltpu.roll(b, d, axis=0), 0.0)
                p = q * pltpu.roll(k, d, axis=0) * jnp.exp(ex)
            a_d = _dot(p.astype(bf16), ones_kc)
            a = a + jnp.where(diag_masks[d], a_d, 0.0)

        st = st_scr[hh]
        o = _dot(a.astype(bf16), v_bf) + _dot_nt((q * jnp.exp(b)).astype(bf16), st.astype(bf16))
        ms = jnp.mean(o * o, axis=-1, keepdims=True)
        y = o * lax.rsqrt(ms + EPS) * onw * _silu(rg_ref[0, rows, lanes])
        yb_ref[0, rows, lanes] = y.astype(yb_ref.dtype)
        k_dec = (k * jnp.exp(b_last - b)).astype(bf16)
        st_scr[hh] = st * jnp.exp(b_last) + _dot_tn(v_bf, k_dec)

    def chunk(c, carry):
        r0 = pl.multiple_of(c * c_len, c_len)
        for hh in range(hp):
            head_chunk(r0, hh)
        return carry

    lax.fori_loop(0, tt // c_len, chunk, 0, unroll=unroll)

    @pl.when(pl.program_id(2) == pl.num_programs(2) - 1)
    def _():
        st_ref[0] = st_scr[...]


def _hgrn_prompt(proj, lb, out_norm, tt, hp=REC_HEADS_PER_STEP, unroll=REC_UNROLL):
    b, t, _ = proj.shape
    base = COL_REC * N_HEADS // hp
    specs = [pl.BlockSpec((1, tt, hp * HEAD_W),
                          functools.partial(lambda i, h, j, c: (i, j, c + h), c=base + c * N_HEADS // hp))
             for c in range(4)]
    return pl.pallas_call(
        functools.partial(_hgrn_kernel, tt=tt, hp=hp, unroll=unroll),
        grid=(b, N_HEADS // hp, t // tt),
        in_specs=specs + [pl.BlockSpec((1, hp * HEAD_W), lambda i, h, j: (0, h)),
                          pl.BlockSpec((1, HEAD_W), lambda i, h, j: (0, 0))],
        out_specs=(pl.BlockSpec((1, tt, hp * HEAD_W), lambda i, h, j: (i, j, h)),
                   pl.BlockSpec((1, hp, HEAD_W, HEAD_W), lambda i, h, j: (i, h, 0, 0))),
        out_shape=(jax.ShapeDtypeStruct((b, t, BRANCH_W), bf16),
                   jax.ShapeDtypeStruct((b, N_HEADS, HEAD_W, HEAD_W), f32)),
        scratch_shapes=[pltpu.VMEM((hp, HEAD_W, HEAD_W), f32)],
        compiler_params=_params(("parallel", "parallel", "arbitrary")),
    )(proj, proj, proj, proj, lb.reshape(1, BRANCH_W), out_norm.reshape(1, HEAD_W))


def _rope_tables(pos):
    half = ROT_DIM // 2
    inv_freq = ROPE_THETA ** (-jnp.arange(half, dtype=f32) * 2.0 / ROT_DIM)
    ang = pos.astype(f32)[:, None] * inv_freq[None, :]
    cos, sin = jnp.cos(ang), jnp.sin(ang)
    t = pos.shape[0]
    one = jnp.ones((t, ATT_HD - ROT_DIM), f32)
    zero = jnp.zeros((t, ATT_HD - ROT_DIM), f32)
    zh = jnp.zeros((t, half), f32)
    cos_t = jnp.concatenate([cos, cos, one], axis=-1)
    sa = jnp.concatenate([-sin, zh, zero], axis=-1)
    sb = jnp.concatenate([zh, sin, zero], axis=-1)
    return tuple(jnp.concatenate([x, x], axis=-1) for x in (cos_t, sa, sb))


def _component_mean_sq(x, grp):
    sq = x * x
    hi = sq.astype(bf16)
    lo = (sq - hi.astype(f32)).astype(bf16)
    return (_dot(lo, grp) + _dot(hi, grp)) * (1.0 / ATT_HD)


def _qk_norm_rope(x, w, grp, cos_t, sa, sb):
    xn = x * lax.rsqrt(_component_mean_sq(x, grp) + EPS) * w
    half = ROT_DIM // 2
    return xn * cos_t + pltpu.roll(xn, HEAD_W - half, axis=1) * sa + pltpu.roll(xn, half, axis=1) * sb


def _group_matrix():
    r = lax.broadcasted_iota(jnp.int32, (HEAD_W, HEAD_W), 0)
    c = lax.broadcasted_iota(jnp.int32, (HEAD_W, HEAD_W), 1)
    return ((r < ATT_HD) == (c < ATT_HD)).astype(bf16)


def _qkprep_kernel(aq_ref, ak_ref, av_ref, qn_ref, kn_ref, cos_ref, sa_ref, sb_ref,
                   kf_ref, kb_ref, vb_ref, qz_ref):
    grp = _group_matrix()
    cos_t, sa, sb = cos_ref[...], sa_ref[...], sb_ref[...]
    lane = lax.broadcasted_iota(jnp.int32, cos_t.shape, 1)
    first = lane < ATT_HD
    scale = Q_SCALE
    for h in range(N_HEADS):
        sl = slice(h * HEAD_W, (h + 1) * HEAD_W)
        kk = _qk_norm_rope(ak_ref[0, :, sl], kn_ref[...], grp, cos_t, sa, sb)
        kf_ref[0, :, sl] = kk
        kb_ref[0, :, sl] = kk.astype(bf16)
        qq = _qk_norm_rope(aq_ref[0, :, sl], qn_ref[...], grp, cos_t, sa, sb) * scale
        qz_ref[0, 0, :, sl] = jnp.where(first, qq, 0.0).astype(bf16)
        qz_ref[0, 1, :, sl] = jnp.where(first, 0.0, qq).astype(bf16)
    vb_ref[0] = av_ref[0].astype(bf16)


def _qk_prep(proj, q_norm, k_norm, tables, tt):
    b, t, _ = proj.shape
    specs = [pl.BlockSpec((1, tt, BRANCH_W), functools.partial(lambda i, j, c: (i, j, c), c=COL_ATT + c))
             for c in range(3)]
    w_spec = pl.BlockSpec((1, HEAD_W), lambda i, j: (0, 0))
    t_spec = pl.BlockSpec((tt, HEAD_W), lambda i, j: (j, 0))
    o_spec = pl.BlockSpec((1, tt, BRANCH_W), lambda i, j: (i, j, 0))
    tile2 = lambda w: jnp.concatenate([w, w]).reshape(1, HEAD_W)
    return pl.pallas_call(
        _qkprep_kernel,
        grid=(b, t // tt),
        in_specs=specs + [w_spec, w_spec, t_spec, t_spec, t_spec],
        out_specs=(o_spec, o_spec, o_spec,
                   pl.BlockSpec((1, 2, tt, BRANCH_W), lambda i, j: (i, 0, j, 0))),
        out_shape=(jax.ShapeDtypeStruct((b, t, BRANCH_W), f32),
                   jax.ShapeDtypeStruct((b, t, BRANCH_W), bf16),
                   jax.ShapeDtypeStruct((b, t, BRANCH_W), bf16),
                   jax.ShapeDtypeStruct((b, 2, t, BRANCH_W), bf16)),
        compiler_params=_params(("parallel", "parallel")),
    )(proj, proj, proj, tile2(q_norm), tile2(k_norm), *tables)


def _subln_gate(o, subln, coef, az):
    ms = jnp.mean(o * o, axis=-1, keepdims=True)
    return o * lax.rsqrt(ms + EPS) * subln * coef * _silu(az)


def _flash_kernel(qz_ref, k_ref, v_ref, az_ref, lam_ref, sub_ref, o_ref, m_scr, acc_scr,
                  *, tq, coef):
    rows = min(FLASH_ROWS, tq)
    qi = pl.program_id(2)
    ki = pl.program_id(3)

    @pl.when(ki == 0)
    def _():
        m_scr[...] = jnp.full_like(m_scr, MASK_VALUE)
        acc_scr[...] = jnp.zeros_like(acc_scr)

    def update(on_diagonal):
        k = k_ref[0]
        v_ones = jnp.concatenate([v_ref[0], jnp.ones((tq, HEAD_W), bf16)], axis=1)
        for r0 in range(0, 2 * tq, rows):
            comp, q0 = divmod(r0, tq)
            sl = slice(r0, r0 + rows)
            s = _dot_nt(qz_ref[0, comp, q0:q0 + rows, :], k)
            if on_diagonal:
                r = lax.broadcasted_iota(jnp.int32, s.shape, 0)
                c = lax.broadcasted_iota(jnp.int32, s.shape, 1)
                s = jnp.where(c <= r + q0, s, MASK_VALUE)
            m_old = m_scr[sl]
            m_new = jnp.maximum(m_old, jnp.max(s, axis=-1, keepdims=True))
            alpha = jnp.exp2(m_old - m_new)
            p = jnp.exp2(s - pltpu.repeat(m_new, tq // HEAD_W, axis=1))
            acc_scr[sl] = pltpu.repeat(alpha, 2, axis=1) * acc_scr[sl] + _dot(p.astype(bf16), v_ones)
            m_scr[sl] = m_new

    @pl.when(ki < qi)
    def _():
        update(False)

    @pl.when(ki == qi)
    def _():
        update(True)
        on = acc_scr[:, 0:HEAD_W] / acc_scr[:, HEAD_W:2 * HEAD_W]
        o = on[0:tq] - lam_ref[...] * on[tq:2 * tq]
        o_ref[0] = _subln_gate(o, sub_ref[...], coef, az_ref[0]).astype(o_ref.dtype)


def _flash_prompt(qz, k_bf, v_bf, proj, lam_row, subln, coef, tq):
    b, t, _ = k_bf.shape
    n = t // tq
    az_col = (COL_ATT + 3) * N_HEADS
    return pl.pallas_call(
        functools.partial(_flash_kernel, tq=tq, coef=coef),
        grid=(b, N_HEADS, n, n),
        in_specs=[pl.BlockSpec((1, 2, tq, HEAD_W), lambda i, h, q, k: (i, 0, q, h)),
                  pl.BlockSpec((1, tq, HEAD_W), lambda i, h, q, k: (i, jnp.minimum(k, q), h)),
                  pl.BlockSpec((1, tq, HEAD_W), lambda i, h, q, k: (i, jnp.minimum(k, q), h)),
                  pl.BlockSpec((1, tq, HEAD_W), lambda i, h, q, k: (i, q, az_col + h)),
                  pl.BlockSpec((1, HEAD_W), lambda i, h, q, k: (0, 0)),
                  pl.BlockSpec((1, HEAD_W), lambda i, h, q, k: (0, 0))],
        out_specs=pl.BlockSpec((1, tq, HEAD_W), lambda i, h, q, k: (i, q, h)),
        out_shape=jax.ShapeDtypeStruct((b, t, BRANCH_W), bf16),
        scratch_shapes=[pltpu.VMEM((2 * tq, HEAD_W), f32),
                        pltpu.VMEM((2 * tq, 2 * HEAD_W), f32)],
        compiler_params=_params(("parallel", "parallel", "parallel", "arbitrary")),
    )(qz, k_bf, v_bf, proj, lam_row, subln.reshape(1, HEAD_W))


def _lane_to_sublane(row):
    r = lax.broadcasted_iota(jnp.int32, (HEAD_W, HEAD_W), 0)
    c = lax.broadcasted_iota(jnp.int32, (HEAD_W, HEAD_W), 1)
    return jnp.sum(jnp.where(r == c, jnp.broadcast_to(row, (HEAD_W, HEAD_W)), 0.0), axis=1, keepdims=True)


def _sample_mix_kernel(p_ref, cbuf_ref, st_ref, cw_ref, lb_ref, onw_ref, qn_ref, kn_ref,
                       cos_ref, sa_ref, sb_ref,
                       ya_ref, yb_ref, nc_ref, nst_ref, kf_ref, qm_ref):
    blk = lambda c: p_ref[0, c * N_HEADS:(c + 1) * N_HEADS, :]
    u = blk(COL_CONV + 2) * blk(COL_CONV)
    y = cw_ref[0] * cbuf_ref[0, 0] + cw_ref[1] * cbuf_ref[0, 1] + cw_ref[2] * u
    ya_ref[0] = (blk(COL_CONV + 1) * y * _silu(blk(COL_CONV + 3))).astype(ya_ref.dtype)
    nc_ref[0, 0] = cbuf_ref[0, 1]
    nc_ref[0, 1] = u
    q, k, g = _rec_gates(blk(COL_REC), blk(COL_REC + 1), lb_ref[...])
    v = blk(COL_REC + 2)
    dec = jnp.exp(g)
    outs = []
    for h in range(N_HEADS):
        st_new = st_ref[0, h] * dec[h:h + 1, :] + _lane_to_sublane(v[h:h + 1, :]) * k[h:h + 1, :]
        nst_ref[0, h] = st_new
        qh = jnp.broadcast_to(q[h:h + 1, :], (8, HEAD_W)).astype(bf16)
        outs.append(_dot_nt(qh, st_new.astype(bf16))[0:1, :])
    o = jnp.concatenate(outs, axis=0)
    ms = jnp.mean(o * o, axis=-1, keepdims=True)
    yb_ref[0] = (o * lax.rsqrt(ms + EPS) * onw_ref[...] * _silu(blk(COL_REC + 3))).astype(yb_ref.dtype)
    grp = _group_matrix()
    cos_t, sa, sb = cos_ref[...], sa_ref[...], sb_ref[...]
    kf_ref[0] = _qk_norm_rope(blk(COL_ATT + 1), kn_ref[...], grp, cos_t, sa, sb)
    qq = _qk_norm_rope(blk(COL_ATT), qn_ref[...], grp, cos_t, sa, sb) * Q_SCALE
    r = lax.broadcasted_iota(jnp.int32, (2 * N_HEADS, HEAD_W), 0)
    lane = lax.broadcasted_iota(jnp.int32, (2 * N_HEADS, HEAD_W), 1)
    for h in range(N_HEADS):
        qh = jnp.broadcast_to(qq[h:h + 1, :], (2 * N_HEADS, HEAD_W))
        keep = r == 2 * h + jnp.where(lane < ATT_HD, 0, 1)
        qm_ref[0, :, h * HEAD_W:(h + 1) * HEAD_W] = jnp.where(keep, qh, 0.0).astype(bf16)


def _sample_mix(proj_s, conv_state, st_t, conv_w, lb, out_norm, q_norm, k_norm, tables):
    nb = proj_s.shape[0]
    rows = N_IN // HEAD_W
    tile2 = lambda w: jnp.concatenate([w, w]).reshape(1, HEAD_W)
    vec = pl.BlockSpec((1, HEAD_W), lambda i: (0, 0))
    hw = (N_HEADS, HEAD_W)
    out_row = lambda dt: (pl.BlockSpec((1,) + hw, lambda i: (i, 0, 0)), jax.ShapeDtypeStruct((nb,) + hw, dt))
    outs = [out_row(f32), out_row(f32),
            (pl.BlockSpec((1, CONV_WIDTH - 1) + hw, lambda i: (i, 0, 0, 0)),
             jax.ShapeDtypeStruct((nb, CONV_WIDTH - 1) + hw, f32)),
            (pl.BlockSpec((1, N_HEADS, HEAD_W, HEAD_W), lambda i: (i, 0, 0, 0)),
             jax.ShapeDtypeStruct((nb, N_HEADS, HEAD_W, HEAD_W), f32)),
            out_row(f32),
            (pl.BlockSpec((1, 2 * N_HEADS, BRANCH_W), lambda i: (i, 0, 0)),
             jax.ShapeDtypeStruct((nb, 2 * N_HEADS, BRANCH_W), bf16))]
    return pl.pallas_call(
        _sample_mix_kernel,
        grid=(nb,),
        in_specs=[pl.BlockSpec((1, rows, HEAD_W), lambda i: (i, 0, 0)),
                  pl.BlockSpec((1, CONV_WIDTH - 1) + hw, lambda i: (i, 0, 0, 0)),
                  pl.BlockSpec((1, N_HEADS, HEAD_W, HEAD_W), lambda i: (i, 0, 0, 0)),
                  pl.BlockSpec((CONV_WIDTH,) + hw, lambda i: (0, 0, 0)),
                  pl.BlockSpec(hw, lambda i: (0, 0)),
                  vec, vec, vec, vec, vec, vec],
        out_specs=tuple(o[0] for o in outs),
        out_shape=tuple(o[1] for o in outs),
        compiler_params=_params(("parallel",)),
    )(proj_s.reshape(nb, rows, HEAD_W), conv_state.reshape((nb, CONV_WIDTH - 1) + hw), st_t,
      conv_w.reshape((CONV_WIDTH,) + hw), lb.reshape(hw), out_norm.reshape(1, HEAD_W),
      tile2(q_norm), tile2(k_norm), *tables)


def _decode_kernel(pt_ref, qm_ref, kn_ref, vn_ref, az_ref, lam_ref, sub_ref, *rest, coef):
    g = PAGES_PER_STEP
    k_refs, v_refs = rest[:g], rest[g:2 * g]
    o_ref, m_scr, l_scr, acc_scr, fin_scr, exp_scr = rest[2 * g:]
    step = pl.program_id(1)
    q = qm_ref[0]
    n_rows = PAGE_SIZE * N_HEADS
    head_shift = int(math.log2(N_HEADS))

    @pl.when(step == 0)
    def _():
        s0 = jnp.sum(q.astype(f32) * kn_ref[0], axis=-1, keepdims=True)
        m_scr[...] = jnp.broadcast_to(s0, m_scr.shape)
        l_scr[...] = jnp.ones_like(l_scr)
        acc_scr[...] = vn_ref[0]
        pos = lax.broadcasted_iota(jnp.int32, (PAGE_SIZE, n_rows), 0)
        col = lax.broadcasted_iota(jnp.int32, (PAGE_SIZE, n_rows), 1)
        exp_scr[...] = jnp.where((col >> head_shift) == pos, 1.0, 0.0).astype(bf16)

    s = jnp.concatenate([_dot(q, k_refs[i][0, 0].astype(bf16)) for i in range(g)], axis=-1)
    m_old = m_scr[...]
    m_new = jnp.maximum(m_old, jnp.max(s, axis=-1, keepdims=True))
    alpha = jnp.exp2(m_old - m_new)
    p = jnp.exp2(s - m_new[:, 0:1])
    l_scr[...] = alpha * l_scr[...] + jnp.sum(p, axis=-1, keepdims=True)
    r = lax.broadcasted_iota(jnp.int32, (2 * N_HEADS, n_rows), 0)
    col = lax.broadcasted_iota(jnp.int32, (2 * N_HEADS, n_rows), 1)
    own_head = (col & (N_HEADS - 1)) == (r >> 1)
    pv = None
    for i in range(g):
        spread = _dot(p[:, i * PAGE_SIZE:(i + 1) * PAGE_SIZE].astype(bf16), exp_scr[...])
        t = _dot(jnp.where(own_head, spread, 0.0).astype(bf16), v_refs[i][0, 0].astype(bf16))
        pv = t if pv is None else pv + t
    acc_scr[...] = alpha * acc_scr[...] + pv
    m_scr[...] = m_new

    @pl.when(step == pl.num_programs(1) - 1)
    def _():
        row = lax.broadcasted_iota(jnp.int32, acc_scr.shape, 0)
        fin_scr[...] = acc_scr[...] / l_scr[...] * jnp.where((row & 1) == 0, 1.0, -lam_ref[...])
        o = fin_scr[pl.ds(0, N_HEADS, stride=2), :] + fin_scr[pl.ds(1, N_HEADS, stride=2), :]
        o_ref[0] = _subln_gate(o, sub_ref[...], coef, az_ref[0]).astype(o_ref.dtype)


def _decode_attention(page_table, qmat, k_new, v_new, az, lam_row, subln, cache_k, cache_v, layer, coef):
    nb, n_pages = page_table.shape
    g = PAGES_PER_STEP
    per_batch = lambda rows, width: pl.BlockSpec((1, rows, width), lambda b, s, pt: (b, 0, 0))
    vec = pl.BlockSpec((1, HEAD_W), lambda b, s, pt: (0, 0))
    page = lambda rows, width: [
        pl.BlockSpec((1, 1, rows, width),
                     functools.partial(lambda b, s, pt, i: (layer, pt[b, s * g + i], 0, 0), i=i))
        for i in range(g)]
    acc = pltpu.VMEM((2 * N_HEADS, HEAD_W), f32)
    grid_spec = pltpu.PrefetchScalarGridSpec(
        num_scalar_prefetch=1,
        grid=(nb, n_pages // g),
        in_specs=[per_batch(2 * N_HEADS, BRANCH_W), per_batch(1, BRANCH_W), per_batch(2 * N_HEADS, HEAD_W),
                  per_batch(N_HEADS, HEAD_W), vec, vec]
        + page(BRANCH_W, PAGE_SIZE) + page(PAGE_SIZE * N_HEADS, HEAD_W),
        out_specs=per_batch(N_HEADS, HEAD_W),
        scratch_shapes=[acc, acc, acc, acc, pltpu.VMEM((PAGE_SIZE, PAGE_SIZE * N_HEADS), bf16)])
    return pl.pallas_call(
        functools.partial(_decode_kernel, coef=coef),
        grid_spec=grid_spec,
        out_shape=jax.ShapeDtypeStruct((nb, N_HEADS, HEAD_W), f32),
        compiler_params=_params(("parallel", "arbitrary")),
    )(page_table, qmat, k_new, v_new, az, lam_row, subln.reshape(1, HEAD_W),
      *([cache_k] * g), *([cache_v] * g))


def kernel(x_prompt, x_sample, cache_k, cache_v, state_conv, state_rec, page_table, norm_w, w_in, conv_w,
           rec_lb_logits, rec_out_norm, q_norm, k_norm, lambda_qk, attn_subln, w_branch, w_out):
    nbp, seq, _ = x_prompt.shape
    nbs = x_sample.shape[0]
    n_pool = cache_k.shape[1]
    lb_all, lam_all = _layer_params(rec_lb_logits, lambda_qk)
    w_in_bf, w_branch_bf, w_out_bf = w_in.astype(bf16), w_branch.astype(bf16), w_out.astype(bf16)
    ck = jnp.transpose(cache_k, (0, 1, 3, 4, 5, 2)).reshape(DEPTH, n_pool, BRANCH_W, PAGE_SIZE)
    cv = cache_v.reshape(DEPTH, n_pool, PAGE_SIZE * N_HEADS, HEAD_W)
    tab_p = _rope_tables(jnp.arange(seq, dtype=jnp.int32))
    tab_s = _rope_tables(jnp.full((1,), page_table.shape[1] * PAGE_SIZE, jnp.int32))

    mp = nbp * seq
    tm = min(1024, mp)
    tt = min(512, seq)
    y_p = x_prompt.reshape(mp, D_MODEL)
    ms = -(-nbs // BF16_SUBLANES) * BF16_SUBLANES
    pad_rows = lambda a: jnp.pad(a, ((0, ms - nbs), (0, 0)))
    y_s = pad_rows(x_sample.reshape(nbs, D_MODEL))
    outs = [[] for _ in range(8)]
    for l in range(DEPTH):
        coef = 1.0 - (0.8 - 0.6 * math.exp(-0.3 * l))
        lam_row = lam_all[l:l + 1]

        proj = _in_proj(y_p, norm_w[l], w_in_bf, l, tm, 1024)
        proj3 = proj.reshape(nbp, seq, N_IN)
        ya, c_p = _conv_prompt(proj3, conv_w[l], tt)
        yb, r_p = _hgrn_prompt(proj3, lb_all[l], rec_out_norm[l], tt)
        k_p, k_bf, v_bf, qz = _qk_prep(proj3, q_norm[l], k_norm[l], tab_p, min(256, seq))
        yc = _flash_prompt(qz, k_bf, v_bf, proj3, lam_row, attn_subln[l], coef, tt)
        mm = _merge(ya.reshape(mp, BRANCH_W), yb.reshape(mp, BRANCH_W), yc.reshape(mp, BRANCH_W),
                    w_branch_bf, l, proj, tm, 512)
        y_p = _out_proj(mm, w_out_bf, l, y_p, tm, 1024)
        v_p = proj3[:, :, (COL_ATT + 2) * BRANCH_W:(COL_ATT + 3) * BRANCH_W]

        proj_pad = _in_proj(y_s, norm_w[l], w_in_bf, l, ms, 1024)
        proj_s = proj_pad[:nbs]
        ya_s, yb_s, c_s, st_s, k_s, qmat = _sample_mix(
            proj_s, state_conv[l], jnp.swapaxes(state_rec[l], -1, -2), conv_w[l], lb_all[l],
            rec_out_norm[l], q_norm[l], k_norm[l], tab_s)
        v_s = proj_s[:, (COL_ATT + 2) * BRANCH_W:(COL_ATT + 3) * BRANCH_W]
        az_s = proj_s[:, (COL_ATT + 3) * BRANCH_W:(COL_ATT + 4) * BRANCH_W]
        v_rows = jnp.repeat(v_s.reshape(nbs, N_HEADS, HEAD_W), 2, axis=1)
        yc_s = _decode_attention(page_table, qmat, k_s.reshape(nbs, 1, BRANCH_W), v_rows,
                                 az_s.reshape(nbs, N_HEADS, HEAD_W), lam_row, attn_subln[l], ck, cv, l, coef)
        mm_s = _merge(pad_rows(ya_s.reshape(nbs, BRANCH_W)).astype(bf16),
                      pad_rows(yb_s.reshape(nbs, BRANCH_W)).astype(bf16),
                      pad_rows(yc_s.reshape(nbs, BRANCH_W)).astype(bf16),
                      w_branch_bf, l, proj_pad, ms, 512)
        y_s = _out_proj(mm_s, w_out_bf, l, y_s, ms, 1024)

        for lst, val in zip(outs, (
                k_p.reshape(nbp, seq, N_HEADS, 2, ATT_HD), v_p.reshape(nbp, seq, N_HEADS, HEAD_W),
                k_s.reshape(nbs, 1, N_HEADS, 2, ATT_HD), v_s.reshape(nbs, 1, N_HEADS, HEAD_W),
                c_p, c_s.reshape(nbs, CONV_WIDTH - 1, BRANCH_W),
                jnp.swapaxes(r_p, -1, -2), jnp.swapaxes(st_s, -1, -2))):
            lst.append(val)

    return (y_p.reshape(nbp, seq, D_MODEL), y_s[:nbs].reshape(nbs, 1, D_MODEL)) + tuple(jnp.stack(o, axis=0) for o in outs)
```

```python
import functools
import math

import jax
import jax.numpy as jnp
import numpy as np
from jax import lax
from jax.experimental import pallas as pl
from jax.experimental.pallas import tpu as pltpu

D_MODEL = 2048
DEPTH = 4
PAST_LEN = 16384
PAGE_SIZE = 128
BRANCH_W = D_MODEL // 2
CONV_WIDTH = 3
N_HEADS = 8
HEAD_W = BRANCH_W // N_HEADS
ATT_HD = HEAD_W // 2
ROT_DIM = ATT_HD // 4
ROPE_THETA = 500000.0
N_BRANCH = 3
EPS = 1e-6
MASK_VALUE = -1e30
N_IN = 12 * BRANCH_W + N_BRANCH * D_MODEL
COL_CONV, COL_REC, COL_ATT, COL_GATE = 0, 4, 8, 12

V7X_VMEM_BYTES = 64 * 1024 * 1024
BF16_SUBLANES = 16
VMEM_LIMIT = 56 * 1024 * 1024
REC_CHUNK = 64
REC_SUB = 2
REC_HEADS_PER_STEP = 4
REC_UNROLL = 2
Q_SCALE = ATT_HD ** -0.5 * math.log2(math.e)
FLASH_ROWS = 256
PAGES_PER_STEP = 8

f32 = jnp.float32
bf16 = jnp.bfloat16


def _params(sem, vmem=VMEM_LIMIT):
    return pltpu.CompilerParams(dimension_semantics=sem, vmem_limit_bytes=vmem)


def _sigmoid(x):
    return 1.0 / (1.0 + jnp.exp(-x))


def _silu(x):
    return x * _sigmoid(x)


def _dot(a, b):
    return jnp.dot(a, b, preferred_element_type=f32)


def _dot_nt(a, b):
    return lax.dot_general(a, b, (((1,), (1,)), ((), ())), preferred_element_type=f32)


def _dot_tn(a, b):
    return lax.dot_general(a, b, (((0,), (0,)), ((), ())), preferred_element_type=f32)


def _param_kernel(logit_ref, lq_ref, lb_ref, lam_ref):
    x = logit_ref[...]
    m = jnp.max(x, axis=0, keepdims=True)
    e = jnp.exp(x - m)
    soft = e / jnp.sum(e, axis=0, keepdims=True)
    run = soft[0:1]
    rows = [run - soft[0:1]]
    for l in range(1, DEPTH):
        run = run + soft[l:l + 1]
        rows.append(run - soft[0:1])
    lb_ref[...] = jnp.concatenate(rows, axis=0)
    lams = []
    for l in range(DEPTH):
        lq = lq_ref[l]
        a = jnp.sum(lq[0:1] * lq[1:2], axis=-1, keepdims=True)
        b = jnp.sum(lq[2:3] * lq[3:4], axis=-1, keepdims=True)
        lam_init = 0.8 - 0.6 * math.exp(-0.3 * l)
        lams.append(jnp.broadcast_to(jnp.exp(a) - jnp.exp(b) + lam_init, (1, HEAD_W)))
    lam_ref[...] = jnp.concatenate(lams, axis=0)


def _layer_params(rec_lb_logits, lambda_qk):
    return pl.pallas_call(
        _param_kernel,
        out_shape=(jax.ShapeDtypeStruct(rec_lb_logits.shape, f32),
                   jax.ShapeDtypeStruct((DEPTH, HEAD_W), f32)),
    )(rec_lb_logits.astype(f32), lambda_qk.astype(f32))


def _inproj_kernel(x_ref, nw_ref, w_ref, o_ref, h_ref):
    @pl.when(pl.program_id(1) == 0)
    def _():
        x = x_ref[...]
        ms = jnp.mean(x * x, axis=-1, keepdims=True)
        h_ref[...] = (x * lax.rsqrt(ms + EPS) * nw_ref[...]).astype(h_ref.dtype)

    o_ref[...] = _dot(h_ref[...], w_ref[...])


def _in_proj(x, norm_w, w_bf, layer, tm, tn):
    m, d = x.shape
    n = w_bf.shape[2]
    return pl.pallas_call(
        _inproj_kernel,
        grid=(m // tm, n // tn),
        in_specs=[pl.BlockSpec((tm, d), lambda i, j: (i, 0)),
                  pl.BlockSpec((1, d), lambda i, j: (0, 0)),
                  pl.BlockSpec((None, d, tn), lambda i, j: (layer, 0, j))],
        out_specs=pl.BlockSpec((tm, tn), lambda i, j: (i, j)),
        out_shape=jax.ShapeDtypeStruct((m, n), f32),
        scratch_shapes=[pltpu.VMEM((tm, d), bf16)],
        compiler_params=_params(("parallel", "arbitrary")),
    )(x, norm_w.reshape(1, d), w_bf)


def _merge_kernel(ya_ref, yb_ref, yc_ref, wb_ref, ga_ref, gb_ref, gc_ref, o_ref):
    acc = None
    for n, (y_ref, g_ref) in enumerate(((ya_ref, ga_ref), (yb_ref, gb_ref), (yc_ref, gc_ref))):
        t = _sigmoid(g_ref[...]) * _dot(y_ref[...], wb_ref[n])
        acc = t if acc is None else acc + t
    o_ref[...] = acc.astype(o_ref.dtype)


def _merge(ya, yb, yc, wb_bf, layer, proj, tm, tn):
    m = ya.shape[0]
    gate0 = COL_GATE * BRANCH_W // tn
    per = D_MODEL // tn
    y_spec = pl.BlockSpec((tm, BRANCH_W), lambda i, j: (i, 0))
    g_specs = [pl.BlockSpec((tm, tn), functools.partial(lambda i, j, n: (i, gate0 + n * per + j), n=n))
               for n in range(N_BRANCH)]
    return pl.pallas_call(
        _merge_kernel,
        grid=(m // tm, D_MODEL // tn),
        in_specs=[y_spec, y_spec, y_spec,
                  pl.BlockSpec((None, N_BRANCH, BRANCH_W, tn), lambda i, j: (layer, 0, 0, j))] + g_specs,
        out_specs=pl.BlockSpec((tm, tn), lambda i, j: (i, j)),
        out_shape=jax.ShapeDtypeStruct((m, D_MODEL), bf16),
        compiler_params=_params(("parallel", "parallel")),
    )(ya, yb, yc, wb_bf, proj, proj, proj)


def _outproj_kernel(m_ref, w_ref, x_ref, o_ref):
    o_ref[...] = x_ref[...] + _dot(m_ref[...], w_ref[...])


def _out_proj(mm, w_bf, layer, x, tm, tn):
    m = x.shape[0]
    return pl.pallas_call(
        _outproj_kernel,
        grid=(m // tm, D_MODEL // tn),
        in_specs=[pl.BlockSpec((tm, D_MODEL), lambda i, j: (i, 0)),
                  pl.BlockSpec((None, D_MODEL, tn), lambda i, j: (layer, 0, j)),
                  pl.BlockSpec((tm, tn), lambda i, j: (i, j))],
        out_specs=pl.BlockSpec((tm, tn), lambda i, j: (i, j)),
        out_shape=jax.ShapeDtypeStruct((m, D_MODEL), f32),
        compiler_params=_params(("parallel", "parallel")),
    )(mm, w_bf, x)


def _conv_kernel(ch_ref, cb_ref, cc_ref, cz_ref, w_ref, ya_ref, nc_ref, carry_ref, *, tt):
    @pl.when(pl.program_id(1) == 0)
    def _():
        carry_ref[...] = jnp.zeros_like(carry_ref)

    u = cc_ref[0] * ch_ref[0]
    prev2 = carry_ref[0:1, :]
    prev1 = carry_ref[1:2, :]
    row = lax.broadcasted_iota(jnp.int32, u.shape, 0)
    u1 = jnp.where(row == 0, prev1, pltpu.roll(u, 1, axis=0))
    u2 = jnp.where(row == 0, prev2, jnp.where(row == 1, prev1, pltpu.roll(u, 2, axis=0)))
    y = w_ref[0:1, :] * u2 + w_ref[1:2, :] * u1 + w_ref[2:3, :] * u
    ya_ref[0] = (cb_ref[0] * y * _silu(cz_ref[0])).astype(ya_ref.dtype)
    tail = u[tt - 2:tt, :]
    carry_ref[0:2, :] = tail
    nc_ref[0] = tail


def _conv_prompt(proj, conv_w, tt):
    b, t, _ = proj.shape
    specs = [pl.BlockSpec((1, tt, BRANCH_W), functools.partial(lambda i, j, c: (i, j, c), c=COL_CONV + c))
             for c in range(4)]
    return pl.pallas_call(
        functools.partial(_conv_kernel, tt=tt),
        grid=(b, t // tt),
        in_specs=specs + [pl.BlockSpec((CONV_WIDTH, BRANCH_W), lambda i, j: (0, 0))],
        out_specs=(pl.BlockSpec((1, tt, BRANCH_W), lambda i, j: (i, j, 0)),
                   pl.BlockSpec((1, CONV_WIDTH - 1, BRANCH_W), lambda i, j: (i, 0, 0))),
        out_shape=(jax.ShapeDtypeStruct((b, t, BRANCH_W), bf16),
                   jax.ShapeDtypeStruct((b, CONV_WIDTH - 1, BRANCH_W), f32)),
        scratch_shapes=[pltpu.VMEM((8, BRANCH_W), f32)],
        compiler_params=_params(("parallel", "arbitrary")),
    )(proj, proj, proj, proj, conv_w)


def _rec_gates(rq, rf, lb):
    e = jnp.exp(-jnp.abs(rf))
    r = 1.0 / (1.0 + e)
    er = e * r
    pos = rf >= 0
    sig = jnp.where(pos, r, er)
    nsig = jnp.where(pos, er, r)
    g = jnp.log(lb + (1.0 - lb) * sig)
    k = (1.0 - lb) * nsig
    return _silu(rq), k, g


def _hgrn_kernel(rq_ref, rf_ref, ri_ref, rg_ref, lb_ref, onw_ref, yb_ref, st_ref, st_scr, *, tt, hp, unroll):
    c_len = REC_CHUNK

    @pl.when(pl.program_id(2) == 0)
    def _():
        st_scr[...] = jnp.zeros_like(st_scr)

    onw = onw_ref[...]
    ri = lax.broadcasted_iota(jnp.int32, (c_len, c_len), 0)
    ci = lax.broadcasted_iota(jnp.int32, (c_len, c_len), 1)
    tri = (ci <= ri).astype(bf16)
    ones_kc = jnp.ones((HEAD_W, c_len), bf16)
    rowk = lax.broadcasted_iota(jnp.int32, (c_len, HEAD_W), 0)
    level_masks = []
    s = REC_SUB
    while s < c_len:
        shift = int(math.log2(2 * s))
        level_masks.append((s, ((((ri ^ ci) >> shift) | ((ri & s) ^ s) | (ci & s)) == 0)))
        s *= 2
    sub_shift = int(math.log2(REC_SUB))
    diag_masks = [((((ri - ci) ^ d) | ((ri >> sub_shift) ^ (ci >> sub_shift))) == 0) for d in range(REC_SUB)]
    valid_rows = [(rowk & (REC_SUB - 1)) >= d for d in range(REC_SUB)]

    def head_chunk(r0, hh):
        rows = pl.ds(r0, c_len)
        lanes = slice(hh * HEAD_W, (hh + 1) * HEAD_W)
        q, k, g = _rec_gates(rq_ref[0, rows, lanes], rf_ref[0, rows, lanes], lb_ref[:, lanes])
        v_bf = ri_ref[0, rows, lanes].astype(bf16)
        g_hi = g.astype(bf16)
        rem = g - g_hi.astype(f32)
        g_mid = rem.astype(bf16)
        g_lo = (rem - g_mid.astype(f32)).astype(bf16)
        b = _dot(tri, g_lo) + _dot(tri, g_mid) + _dot(tri, g_hi)
        b_last = b[c_len - 1:c_len, :]

        a = jnp.zeros((c_len, c_len), f32)
        for s, mask in level_masks:
            refs = [jnp.broadcast_to(b[m + s - 1:m + s, :], (2 * s, HEAD_W)) for m in range(0, c_len, 2 * s)]
            z = jnp.exp(-jnp.abs(b - jnp.concatenate(refs, axis=0)))
            a = a + jnp.where(mask, _dot_nt((q * z).astype(bf16), (k * z).astype(bf16)), 0.0)
        for d in range(REC_SUB):
            if d == 0:
                p = q * k
            else:
                ex = jnp.where(valid_rows[d], b - pltpu.roll(b, d, axis=0), 0.0)
                p = q * pltpu.roll(k, d, axis=0) * jnp.exp(ex)
            a_d = _dot(p.astype(bf16), ones_kc)
            a = a + jnp.where(diag_masks[d], a_d, 0.0)

        st = st_scr[hh]
        o = _dot(a.astype(bf16), v_bf) + _dot_nt((q * jnp.exp(b)).astype(bf16), st.astype(bf16))
        ms = jnp.mean(o * o, axis=-1, keepdims=True)
        y = o * lax.rsqrt(ms + EPS) * onw * _silu(rg_ref[0, rows, lanes])
        yb_ref[0, rows, lanes] = y.astype(yb_ref.dtype)
        k_dec = (k * jnp.exp(b_last - b)).astype(bf16)
        st_scr[hh] = st * jnp.exp(b_last) + _dot_tn(v_bf, k_dec)

    def chunk(c, carry):
        r0 = pl.multiple_of(c * c_len, c_len)
        for hh in range(hp):
            head_chunk(r0, hh)
        return carry

    lax.fori_loop(0, tt // c_len, chunk, 0, unroll=unroll)

    @pl.when(pl.program_id(2) == pl.num_programs(2) - 1)
    def _():
        st_ref[0] = st_scr[...]


def _hgrn_prompt(proj, lb, out_norm, tt, hp=REC_HEADS_PER_STEP, unroll=REC_UNROLL):
    b, t, _ = proj.shape
    base = COL_REC * N_HEADS // hp
    specs = [pl.BlockSpec((1, tt, hp * HEAD_W),
                          functools.partial(lambda i, h, j, c: (i, j, c + h), c=base + c * N_HEADS // hp))
             for c in range(4)]
    return pl.pallas_call(
        functools.partial(_hgrn_kernel, tt=tt, hp=hp, unroll=unroll),
        grid=(b, N_HEADS // hp, t // tt),
        in_specs=specs + [pl.BlockSpec((1, hp * HEAD_W), lambda i, h, j: (0, h)),
                          pl.BlockSpec((1, HEAD_W), lambda i, h, j: (0, 0))],
        out_specs=(pl.BlockSpec((1, tt, hp * HEAD_W), lambda i, h, j: (i, j, h)),
                   pl.BlockSpec((1, hp, HEAD_W, HEAD_W), lambda i, h, j: (i, h, 0, 0))),
        out_shape=(jax.ShapeDtypeStruct((b, t, BRANCH_W), bf16),
                   jax.ShapeDtypeStruct((b, N_HEADS, HEAD_W, HEAD_W), f32)),
        scratch_shapes=[pltpu.VMEM((hp, HEAD_W, HEAD_W), f32)],
        compiler_params=_params(("parallel", "parallel", "arbitrary")),
    )(proj, proj, proj, proj, lb.reshape(1, BRANCH_W), out_norm.reshape(1, HEAD_W))


def _rope_tables(pos):
    half = ROT_DIM // 2
    inv_freq = ROPE_THETA ** (-jnp.arange(half, dtype=f32) * 2.0 / ROT_DIM)
    ang = pos.astype(f32)[:, None] * inv_freq[None, :]
    cos, sin = jnp.cos(ang), jnp.sin(ang)
    t = pos.shape[0]
    one = jnp.ones((t, ATT_HD - ROT_DIM), f32)
    zero = jnp.zeros((t, ATT_HD - ROT_DIM), f32)
    zh = jnp.zeros((t, half), f32)
    cos_t = jnp.concatenate([cos, cos, one], axis=-1)
    sa = jnp.concatenate([-sin, zh, zero], axis=-1)
    sb = jnp.concatenate([zh, sin, zero], axis=-1)
    return tuple(jnp.concatenate([x, x], axis=-1) for x in (cos_t, sa, sb))


def _component_mean_sq(x, grp):
    sq = x * x
    hi = sq.astype(bf16)
    lo = (sq - hi.astype(f32)).astype(bf16)
    return (_dot(lo, grp) + _dot(hi, grp)) * (1.0 / ATT_HD)


def _qk_norm_rope(x, w, grp, cos_t, sa, sb):
    xn = x * lax.rsqrt(_component_mean_sq(x, grp) + EPS) * w
    half = ROT_DIM // 2
    return xn * cos_t + pltpu.roll(xn, HEAD_W - half, axis=1) * sa + pltpu.roll(xn, half, axis=1) * sb


def _group_matrix():
    r = lax.broadcasted_iota(jnp.int32, (HEAD_W, HEAD_W), 0)
    c = lax.broadcasted_iota(jnp.int32, (HEAD_W, HEAD_W), 1)
    return ((r < ATT_HD) == (c < ATT_HD)).astype(bf16)


def _qkprep_kernel(aq_ref, ak_ref, av_ref, qn_ref, kn_ref, cos_ref, sa_ref, sb_ref,
                   kf_ref, kb_ref, vb_ref, qz_ref):
    grp = _group_matrix()
    cos_t, sa, sb = cos_ref[...], sa_ref[...], sb_ref[...]
    lane = lax.broadcasted_iota(jnp.int32, cos_t.shape, 1)
    first = lane < ATT_HD
    scale = Q_SCALE
    for h in range(N_HEADS):
        sl = slice(h * HEAD_W, (h + 1) * HEAD_W)
        kk = _qk_norm_rope(ak_ref[0, :, sl], kn_ref[...], grp, cos_t, sa, sb)
        kf_ref[0, :, sl] = kk
        kb_ref[0, :, sl] = kk.astype(bf16)
        qq = _qk_norm_rope(aq_ref[0, :, sl], qn_ref[...], grp, cos_t, sa, sb) * scale
        qz_ref[0, 0, :, sl] = jnp.where(first, qq, 0.0).astype(bf16)
        qz_ref[0, 1, :, sl] = jnp.where(first, 0.0, qq).astype(bf16)
    vb_ref[0] = av_ref[0].astype(bf16)


def _qk_prep(proj, q_norm, k_norm, tables, tt):
    b, t, _ = proj.shape
    specs = [pl.BlockSpec((1, tt, BRANCH_W), functools.partial(lambda i, j, c: (i, j, c), c=COL_ATT + c))
             for c in range(3)]
    w_spec = pl.BlockSpec((1, HEAD_W), lambda i, j: (0, 0))
    t_spec = pl.BlockSpec((tt, HEAD_W), lambda i, j: (j, 0))
    o_spec = pl.BlockSpec((1, tt, BRANCH_W), lambda i, j: (i, j, 0))
    tile2 = lambda w: jnp.concatenate([w, w]).reshape(1, HEAD_W)
    return pl.pallas_call(
        _qkprep_kernel,
        grid=(b, t // tt),
        in_specs=specs + [w_spec, w_spec, t_spec, t_spec, t_spec],
        out_specs=(o_spec, o_spec, o_spec,
                   pl.BlockSpec((1, 2, tt, BRANCH_W), lambda i, j: (i, 0, j, 0))),
        out_shape=(jax.ShapeDtypeStruct((b, t, BRANCH_W), f32),
                   jax.ShapeDtypeStruct((b, t, BRANCH_W), bf16),
                   jax.ShapeDtypeStruct((b, t, BRANCH_W), bf16),
                   jax.ShapeDtypeStruct((b, 2, t, BRANCH_W), bf16)),
        compiler_params=_params(("parallel", "parallel")),
    )(proj, proj, proj, tile2(q_norm), tile2(k_norm), *tables)


def _subln_gate(o, subln, coef, az):
    ms = jnp.mean(o * o, axis=-1, keepdims=True)
    return o * lax.rsqrt(ms + EPS) * subln * coef * _silu(az)


def _flash_kernel(qz_ref, k_ref, v_ref, az_ref, lam_ref, sub_ref, o_ref, m_scr, acc_scr,
                  *, tq, coef):
    rows = min(FLASH_ROWS, tq)
    qi = pl.program_id(2)
    m_scr[...] = jnp.full_like(m_scr, MASK_VALUE)
    acc_scr[...] = jnp.zeros_like(acc_scr)

    def update(ki, on_diagonal):
        keys = pl.ds(pl.multiple_of(ki * tq, tq), tq)
        k = k_ref[0, keys, :]
        v_ones = jnp.concatenate([v_ref[0, keys, :], jnp.ones((tq, HEAD_W), bf16)], axis=1)
        for r0 in range(0, 2 * tq, rows):
            comp, q0 = divmod(r0, tq)
            sl = slice(r0, r0 + rows)
            s = _dot_nt(qz_ref[0, comp, q0:q0 + rows, :], k)
            if on_diagonal:
                r = lax.broadcasted_iota(jnp.int32, s.shape, 0)
                c = lax.broadcasted_iota(jnp.int32, s.shape, 1)
                s = jnp.where(c <= r + q0, s, MASK_VALUE)
            m_old = m_scr[sl]
            m_new = jnp.maximum(m_old, jnp.max(s, axis=-1, keepdims=True))
            alpha = jnp.exp2(m_old - m_new)
            p = jnp.exp2(s - jnp.concatenate([m_new] * (tq // HEAD_W), axis=1))
            acc_scr[sl] = jnp.concatenate([alpha, alpha], axis=1) * acc_scr[sl] + _dot(p.astype(bf16), v_ones)
            m_scr[sl] = m_new

    def below_diagonal(ki, carry):
        update(ki, False)
        return carry

    lax.fori_loop(0, qi, below_diagonal, 0)
    update(qi, True)
    on = acc_scr[:, 0:HEAD_W] / acc_scr[:, HEAD_W:2 * HEAD_W]
    o = on[0:tq] - lam_ref[...] * on[tq:2 * tq]
    o_ref[0] = _subln_gate(o, sub_ref[...], coef, az_ref[0]).astype(o_ref.dtype)


def _flash_prompt(qz, k_bf, v_bf, proj, lam_row, subln, coef, tq):
    b, t, _ = k_bf.shape
    n = t // tq
    az_col = (COL_ATT + 3) * N_HEADS
    return pl.pallas_call(
        functools.partial(_flash_kernel, tq=tq, coef=coef),
        grid=(b, N_HEADS, n),
        in_specs=[pl.BlockSpec((1, 2, tq, HEAD_W), lambda i, h, q: (i, 0, q, h)),
                  pl.BlockSpec((1, t, HEAD_W), lambda i, h, q: (i, 0, h)),
                  pl.BlockSpec((1, t, HEAD_W), lambda i, h, q: (i, 0, h)),
                  pl.BlockSpec((1, tq, HEAD_W), lambda i, h, q: (i, q, az_col + h)),
                  pl.BlockSpec((1, HEAD_W), lambda i, h, q: (0, 0)),
                  pl.BlockSpec((1, HEAD_W), lambda i, h, q: (0, 0))],
        out_specs=pl.BlockSpec((1, tq, HEAD_W), lambda i, h, q: (i, q, h)),
        out_shape=jax.ShapeDtypeStruct((b, t, BRANCH_W), bf16),
        scratch_shapes=[pltpu.VMEM((2 * tq, HEAD_W), f32),
                        pltpu.VMEM((2 * tq, 2 * HEAD_W), f32)],
        compiler_params=_params(("parallel", "parallel", "parallel")),
    )(qz, k_bf, v_bf, proj, lam_row, subln.reshape(1, HEAD_W))


def _lane_to_sublane(row):
    r = lax.broadcasted_iota(jnp.int32, (HEAD_W, HEAD_W), 0)
    c = lax.broadcasted_iota(jnp.int32, (HEAD_W, HEAD_W), 1)
    return jnp.sum(jnp.where(r == c, jnp.broadcast_to(row, (HEAD_W, HEAD_W)), 0.0), axis=1, keepdims=True)


def _sample_mix_kernel(p_ref, cbuf_ref, st_ref, cw_ref, lb_ref, onw_ref, qn_ref, kn_ref,
                       cos_ref, sa_ref, sb_ref,
                       ya_ref, yb_ref, nc_ref, nst_ref, kf_ref, qm_ref):
    blk = lambda c: p_ref[0, c * N_HEADS:(c + 1) * N_HEADS, :]
    u = blk(COL_CONV + 2) * blk(COL_CONV)
    y = cw_ref[0] * cbuf_ref[0, 0] + cw_ref[1] * cbuf_ref[0, 1] + cw_ref[2] * u
    ya_ref[0] = (blk(COL_CONV + 1) * y * _silu(blk(COL_CONV + 3))).astype(ya_ref.dtype)
    nc_ref[0, 0] = cbuf_ref[0, 1]
    nc_ref[0, 1] = u
    q, k, g = _rec_gates(blk(COL_REC), blk(COL_REC + 1), lb_ref[...])
    v = blk(COL_REC + 2)
    dec = jnp.exp(g)
    outs = []
    for h in range(N_HEADS):
        st_new = st_ref[0, h] * dec[h:h + 1, :] + _lane_to_sublane(v[h:h + 1, :]) * k[h:h + 1, :]
        nst_ref[0, h] = st_new
        qh = jnp.broadcast_to(q[h:h + 1, :], (8, HEAD_W)).astype(bf16)
        outs.append(_dot_nt(qh, st_new.astype(bf16))[0:1, :])
    o = jnp.concatenate(outs, axis=0)
    ms = jnp.mean(o * o, axis=-1, keepdims=True)
    yb_ref[0] = (o * lax.rsqrt(ms + EPS) * onw_ref[...] * _silu(blk(COL_REC + 3))).astype(yb_ref.dtype)
    grp = _group_matrix()
    cos_t, sa, sb = cos_ref[...], sa_ref[...], sb_ref[...]
    kf_ref[0] = _qk_norm_rope(blk(COL_ATT + 1), kn_ref[...], grp, cos_t, sa, sb)
    qq = _qk_norm_rope(blk(COL_ATT), qn_ref[...], grp, cos_t, sa, sb) * Q_SCALE
    r = lax.broadcasted_iota(jnp.int32, (2 * N_HEADS, HEAD_W), 0)
    lane = lax.broadcasted_iota(jnp.int32, (2 * N_HEADS, HEAD_W), 1)
    for h in range(N_HEADS):
        qh = jnp.broadcast_to(qq[h:h + 1, :], (2 * N_HEADS, HEAD_W))
        keep = r == 2 * h + jnp.where(lane < ATT_HD, 0, 1)
        qm_ref[0, :, h * HEAD_W:(h + 1) * HEAD_W] = jnp.where(keep, qh, 0.0).astype(bf16)


def _sample_mix(proj_s, conv_state, st_t, conv_w, lb, out_norm, q_norm, k_norm, tables):
    nb = proj_s.shape[0]
    rows = N_IN // HEAD_W
    tile2 = lambda w: jnp.concatenate([w, w]).reshape(1, HEAD_W)
    vec = pl.BlockSpec((1, HEAD_W), lambda i: (0, 0))
    hw = (N_HEADS, HEAD_W)
    out_row = lambda dt: (pl.BlockSpec((1,) + hw, lambda i: (i, 0, 0)), jax.ShapeDtypeStruct((nb,) + hw, dt))
    outs = [out_row(f32), out_row(f32),
            (pl.BlockSpec((1, CONV_WIDTH - 1) + hw, lambda i: (i, 0, 0, 0)),
             jax.ShapeDtypeStruct((nb, CONV_WIDTH - 1) + hw, f32)),
            (pl.BlockSpec((1, N_HEADS, HEAD_W, HEAD_W), lambda i: (i, 0, 0, 0)),
             jax.ShapeDtypeStruct((nb, N_HEADS, HEAD_W, HEAD_W), f32)),
            out_row(f32),
            (pl.BlockSpec((1, 2 * N_HEADS, BRANCH_W), lambda i: (i, 0, 0)),
             jax.ShapeDtypeStruct((nb, 2 * N_HEADS, BRANCH_W), bf16))]
    return pl.pallas_call(
        _sample_mix_kernel,
        grid=(nb,),
        in_specs=[pl.BlockSpec((1, rows, HEAD_W), lambda i: (i, 0, 0)),
                  pl.BlockSpec((1, CONV_WIDTH - 1) + hw, lambda i: (i, 0, 0, 0)),
                  pl.BlockSpec((1, N_HEADS, HEAD_W, HEAD_W), lambda i: (i, 0, 0, 0)),
                  pl.BlockSpec((CONV_WIDTH,) + hw, lambda i: (0, 0, 0)),
                  pl.BlockSpec(hw, lambda i: (0, 0)),
                  vec, vec, vec, vec, vec, vec],
        out_specs=tuple(o[0] for o in outs),
        out_shape=tuple(o[1] for o in outs),
        compiler_params=_params(("parallel",)),
    )(proj_s.reshape(nb, rows, HEAD_W), conv_state.reshape((nb, CONV_WIDTH - 1) + hw), st_t,
      conv_w.reshape((CONV_WIDTH,) + hw), lb.reshape(hw), out_norm.reshape(1, HEAD_W),
      tile2(q_norm), tile2(k_norm), *tables)


def _decode_kernel(pt_ref, qm_ref, kn_ref, vn_ref, az_ref, lam_ref, sub_ref, *rest, coef):
    g = PAGES_PER_STEP
    k_refs, v_refs = rest[:g], rest[g:2 * g]
    o_ref, m_scr, l_scr, acc_scr, fin_scr, exp_scr = rest[2 * g:]
    step = pl.program_id(1)
    q = qm_ref[0]
    n_rows = PAGE_SIZE * N_HEADS
    head_shift = int(math.log2(N_HEADS))

    @pl.when(step == 0)
    def _():
        s0 = jnp.sum(q.astype(f32) * kn_ref[0], axis=-1, keepdims=True)
        m_scr[...] = jnp.broadcast_to(s0, m_scr.shape)
        l_scr[...] = jnp.ones_like(l_scr)
        acc_scr[...] = vn_ref[0]
        pos = lax.broadcasted_iota(jnp.int32, (PAGE_SIZE, n_rows), 0)
        col = lax.broadcasted_iota(jnp.int32, (PAGE_SIZE, n_rows), 1)
        exp_scr[...] = jnp.where((col >> head_shift) == pos, 1.0, 0.0).astype(bf16)

    s = jnp.concatenate([_dot(q, k_refs[i][0, 0].astype(bf16)) for i in range(g)], axis=-1)
    m_old = m_scr[...]
    m_new = jnp.maximum(m_old, jnp.max(s, axis=-1, keepdims=True))
    alpha = jnp.exp2(m_old - m_new)
    p = jnp.exp2(s - jnp.concatenate([m_new] * g, axis=1))
    l_scr[...] = alpha * l_scr[...] + jnp.sum(p, axis=-1, keepdims=True)
    n_q = 2 * N_HEADS
    p_bf = p.astype(bf16)
    stacked = jnp.concatenate([p_bf[:, i * PAGE_SIZE:(i + 1) * PAGE_SIZE] for i in range(g)], axis=0)
    r = lax.broadcasted_iota(jnp.int32, (g * n_q, n_rows), 0)
    col = lax.broadcasted_iota(jnp.int32, (g * n_q, n_rows), 1)
    own_head = (col & (N_HEADS - 1)) == ((r & (n_q - 1)) >> 1)
    spread = jnp.where(own_head, _dot(stacked, exp_scr[...]), 0.0).astype(bf16)
    pv = None
    for i in range(g):
        t = _dot(spread[i * n_q:(i + 1) * n_q], v_refs[i][0, 0].astype(bf16))
        pv = t if pv is None else pv + t
    acc_scr[...] = alpha * acc_scr[...] + pv
    m_scr[...] = m_new

    @pl.when(step == pl.num_programs(1) - 1)
    def _():
        row = lax.broadcasted_iota(jnp.int32, acc_scr.shape, 0)
        fin_scr[...] = acc_scr[...] / l_scr[...] * jnp.where((row & 1) == 0, 1.0, -lam_ref[...])
        o = fin_scr[pl.ds(0, N_HEADS, stride=2), :] + fin_scr[pl.ds(1, N_HEADS, stride=2), :]
        o_ref[0] = _subln_gate(o, sub_ref[...], coef, az_ref[0]).astype(o_ref.dtype)


def _decode_attention(page_table, qmat, k_new, v_new, az, lam_row, subln, cache_k, cache_v, layer, coef):
    nb, n_pages = page_table.shape
    g = PAGES_PER_STEP
    per_batch = lambda rows, width: pl.BlockSpec((1, rows, width), lambda b, s, pt: (b, 0, 0))
    vec = pl.BlockSpec((1, HEAD_W), lambda b, s, pt: (0, 0))
    page = lambda rows, width: [
        pl.BlockSpec((1, 1, rows, width),
                     functools.partial(lambda b, s, pt, i: (layer, pt[b, s * g + i], 0, 0), i=i))
        for i in range(g)]
    acc = pltpu.VMEM((2 * N_HEADS, HEAD_W), f32)
    grid_spec = pltpu.PrefetchScalarGridSpec(
        num_scalar_prefetch=1,
        grid=(nb, n_pages // g),
        in_specs=[per_batch(2 * N_HEADS, BRANCH_W), per_batch(1, BRANCH_W), per_batch(2 * N_HEADS, HEAD_W),
                  per_batch(N_HEADS, HEAD_W), vec, vec]
        + page(BRANCH_W, PAGE_SIZE) + page(PAGE_SIZE * N_HEADS, HEAD_W),
        out_specs=per_batch(N_HEADS, HEAD_W),
        scratch_shapes=[acc, acc, acc, acc, pltpu.VMEM((PAGE_SIZE, PAGE_SIZE * N_HEADS), bf16)])
    return pl.pallas_call(
        functools.partial(_decode_kernel, coef=coef),
        grid_spec=grid_spec,
        out_shape=jax.ShapeDtypeStruct((nb, N_HEADS, HEAD_W), f32),
        compiler_params=_params(("parallel", "arbitrary")),
    )(page_table, qmat, k_new, v_new, az, lam_row, subln.reshape(1, HEAD_W),
      *([cache_k] * g), *([cache_v] * g))


def kernel(x_prompt, x_sample, cache_k, cache_v, state_conv, state_rec, page_table, norm_w, w_in, conv_w,
           rec_lb_logits, rec_out_norm, q_norm, k_norm, lambda_qk, attn_subln, w_branch, w_out):
    nbp, seq, _ = x_prompt.shape
    nbs = x_sample.shape[0]
    n_pool = cache_k.shape[1]
    lb_all, lam_all = _layer_params(rec_lb_logits, lambda_qk)
    w_in_bf, w_branch_bf, w_out_bf = w_in.astype(bf16), w_branch.astype(bf16), w_out.astype(bf16)
    ck = jnp.transpose(cache_k, (0, 1, 3, 4, 5, 2)).reshape(DEPTH, n_pool, BRANCH_W, PAGE_SIZE)
    cv = cache_v.reshape(DEPTH, n_pool, PAGE_SIZE * N_HEADS, HEAD_W)
    tab_p = _rope_tables(jnp.arange(seq, dtype=jnp.int32))
    tab_s = _rope_tables(jnp.full((1,), page_table.shape[1] * PAGE_SIZE, jnp.int32))

    mp = nbp * seq
    tm = min(1024, mp)
    tt = min(512, seq)
    y_p = x_prompt.reshape(mp, D_MODEL)
    ms = -(-nbs // BF16_SUBLANES) * BF16_SUBLANES
    pad_rows = lambda a: jnp.pad(a, ((0, ms - nbs), (0, 0)))
    y_s = pad_rows(x_sample.reshape(nbs, D_MODEL))
    outs = [[] for _ in range(8)]
    for l in range(DEPTH):
        coef = 1.0 - (0.8 - 0.6 * math.exp(-0.3 * l))
        lam_row = lam_all[l:l + 1]

        proj = _in_proj(y_p, norm_w[l], w_in_bf, l, tm, 1024)
        proj3 = proj.reshape(nbp, seq, N_IN)
        ya, c_p = _conv_prompt(proj3, conv_w[l], tt)
        yb, r_p = _hgrn_prompt(proj3, lb_all[l], rec_out_norm[l], tt)
        k_p, k_bf, v_bf, qz = _qk_prep(proj3, q_norm[l], k_norm[l], tab_p, min(256, seq))
        yc = _flash_prompt(qz, k_bf, v_bf, proj3, lam_row, attn_subln[l], coef, tt)
        mm = _merge(ya.reshape(mp, BRANCH_W), yb.reshape(mp, BRANCH_W), yc.reshape(mp, BRANCH_W),
                    w_branch_bf, l, proj, tm, 512)
        y_p = _out_proj(mm, w_out_bf, l, y_p, tm, 1024)
        v_p = proj3[:, :, (COL_ATT + 2) * BRANCH_W:(COL_ATT + 3) * BRANCH_W]

        proj_pad = _in_proj(y_s, norm_w[l], w_in_bf, l, ms, 1024)
        proj_s = proj_pad[:nbs]
        ya_s, yb_s, c_s, st_s, k_s, qmat = _sample_mix(
            proj_s, state_conv[l], jnp.swapaxes(state_rec[l], -1, -2), conv_w[l], lb_all[l],
            rec_out_norm[l], q_norm[l], k_norm[l], tab_s)
        v_s = proj_s[:, (COL_ATT + 2) * BRANCH_W:(COL_ATT + 3) * BRANCH_W]
        az_s = proj_s[:, (COL_ATT + 3) * BRANCH_W:(COL_ATT + 4) * BRANCH_W]
        v_rows = jnp.repeat(v_s.reshape(nbs, N_HEADS, HEAD_W), 2, axis=1)
        yc_s = _decode_attention(page_table, qmat, k_s.reshape(nbs, 1, BRANCH_W), v_rows,
                                 az_s.reshape(nbs, N_HEADS, HEAD_W), lam_row, attn_subln[l], ck, cv, l, coef)
        mm_s = _merge(pad_rows(ya_s.reshape(nbs, BRANCH_W)).astype(bf16),
                      pad_rows(yb_s.reshape(nbs, BRANCH_W)).astype(bf16),
                      pad_rows(yc_s.reshape(nbs, BRANCH_W)).astype(bf16),
                      w_branch_bf, l, proj_pad, ms, 512)
        y_s = _out_proj(mm_s, w_out_bf, l, y_s, ms, 1024)

        for lst, val in zip(outs, (
                k_p.reshape(nbp, seq, N_HEADS, 2, ATT_HD), v_p.reshape(nbp, seq, N_HEADS, HEAD_W),
                k_s.reshape(nbs, 1, N_HEADS, 2, ATT_HD), v_s.reshape(nbs, 1, N_HEADS, HEAD_W),
                c_p, c_s.reshape(nbs, CONV_WIDTH - 1, BRANCH_W),
                jnp.swapaxes(r_p, -1, -2), jnp.swapaxes(st_s, -1, -2))):
            lst.append(val)

    return (y_p.reshape(nbp, seq, D_MODEL), y_s[:nbs].reshape(nbs, 1, D_MODEL)) + tuple(jnp.stack(o, axis=0) for o in outs)
```

```python
import functools
import math

import jax
import jax.numpy as jnp
import numpy as np
from jax import lax
from jax.experimental import pallas as pl
from jax.experimental.pallas import tpu as pltpu

D_MODEL = 2048
DEPTH = 4
PAST_LEN = 16384
PAGE_SIZE = 128
BRANCH_W = D_MODEL // 2
CONV_WIDTH = 3
N_HEADS = 8
HEAD_W = BRANCH_W // N_HEADS
ATT_HD = HEAD_W // 2
ROT_DIM = ATT_HD // 4
ROPE_THETA = 500000.0
N_BRANCH = 3
EPS = 1e-6
MASK_VALUE = -1e30
N_IN = 12 * BRANCH_W + N_BRANCH * D_MODEL
COL_CONV, COL_REC, COL_ATT, COL_GATE = 0, 4, 8, 12

V7X_VMEM_BYTES = 64 * 1024 * 1024
BF16_SUBLANES = 16
VMEM_LIMIT = 56 * 1024 * 1024
REC_CHUNK = 128
REC_SUB = 2
REC_HEADS_PER_STEP = 8
REC_UNROLL = 1
Q_SCALE = ATT_HD ** -0.5 * math.log2(math.e)
FLASH_ROWS = 256
FLASH_HEADS_PER_STEP = 8
PAGES_PER_STEP = 8

f32 = jnp.float32
bf16 = jnp.bfloat16


def _params(sem, vmem=VMEM_LIMIT):
    return pltpu.CompilerParams(dimension_semantics=sem, vmem_limit_bytes=vmem)


def _sigmoid(x):
    return 1.0 / (1.0 + jnp.exp(-x))


def _silu(x):
    return x * _sigmoid(x)


def _dot(a, b):
    return jnp.dot(a, b, preferred_element_type=f32)


def _dot_nt(a, b):
    return lax.dot_general(a, b, (((1,), (1,)), ((), ())), preferred_element_type=f32)


def _dot_tn(a, b):
    return lax.dot_general(a, b, (((0,), (0,)), ((), ())), preferred_element_type=f32)


def _param_kernel(logit_ref, lq_ref, lb_ref, lam_ref):
    x = logit_ref[...]
    m = jnp.max(x, axis=0, keepdims=True)
    e = jnp.exp(x - m)
    soft = e / jnp.sum(e, axis=0, keepdims=True)
    run = soft[0:1]
    rows = [run - soft[0:1]]
    for l in range(1, DEPTH):
        run = run + soft[l:l + 1]
        rows.append(run - soft[0:1])
    lb_ref[...] = jnp.concatenate(rows, axis=0)
    lams = []
    for l in range(DEPTH):
        lq = lq_ref[l]
        a = jnp.sum(lq[0:1] * lq[1:2], axis=-1, keepdims=True)
        b = jnp.sum(lq[2:3] * lq[3:4], axis=-1, keepdims=True)
        lam_init = 0.8 - 0.6 * math.exp(-0.3 * l)
        lams.append(jnp.broadcast_to(jnp.exp(a) - jnp.exp(b) + lam_init, (1, HEAD_W)))
    lam_ref[...] = jnp.concatenate(lams, axis=0)


def _layer_params(rec_lb_logits, lambda_qk):
    return pl.pallas_call(
        _param_kernel,
        out_shape=(jax.ShapeDtypeStruct(rec_lb_logits.shape, f32),
                   jax.ShapeDtypeStruct((DEPTH, HEAD_W), f32)),
    )(rec_lb_logits.astype(f32), lambda_qk.astype(f32))


def _rmsnorm_kernel(x_ref, nw_ref, h_ref):
    x = x_ref[...]
    ms = jnp.mean(x * x, axis=-1, keepdims=True)
    h_ref[...] = (x * lax.rsqrt(ms + EPS) * nw_ref[...]).astype(h_ref.dtype)


def _rmsnorm_bf16(x, norm_w, tm):
    m, d = x.shape
    return pl.pallas_call(
        _rmsnorm_kernel,
        grid=(m // tm,),
        in_specs=[pl.BlockSpec((tm, d), lambda i: (i, 0)), pl.BlockSpec((1, d), lambda i: (0, 0))],
        out_specs=pl.BlockSpec((tm, d), lambda i: (i, 0)),
        out_shape=jax.ShapeDtypeStruct((m, d), bf16),
        compiler_params=_params(("parallel",)),
    )(x, norm_w.reshape(1, d))


def _inproj_kernel(hp_ref, hs_ref, w_ref, op_ref, os_ref, wb_scr):
    @pl.when(pl.program_id(1) == 0)
    def _():
        wb_scr[...] = w_ref[...].astype(bf16)
        os_ref[...] = _dot(hs_ref[...], wb_scr[...])

    op_ref[...] = _dot(hp_ref[...], wb_scr[...])


def _in_proj(h_p, h_s, w, layer, tm, tn):
    m, d = h_p.shape
    ms = h_s.shape[0]
    n = w.shape[2]
    return pl.pallas_call(
        _inproj_kernel,
        grid=(n // tn, m // tm),
        in_specs=[pl.BlockSpec((tm, d), lambda j, i: (i, 0)),
                  pl.BlockSpec((ms, d), lambda j, i: (0, 0)),
                  pl.BlockSpec((None, d, tn), lambda j, i: (layer, 0, j))],
        out_specs=(pl.BlockSpec((tm, tn), lambda j, i: (i, j)),
                   pl.BlockSpec((ms, tn), lambda j, i: (0, j))),
        out_shape=(jax.ShapeDtypeStruct((m, n), f32), jax.ShapeDtypeStruct((ms, n), f32)),
        scratch_shapes=[pltpu.VMEM((d, tn), bf16)],
        compiler_params=_params(("parallel", "arbitrary")),
    )(h_p, h_s, w)


def _merge_kernel(ya_ref, yb_ref, yc_ref, wb_ref, ga_ref, gb_ref, gc_ref, o_ref):
    acc = None
    for n, (y_ref, g_ref) in enumerate(((ya_ref, ga_ref), (yb_ref, gb_ref), (yc_ref, gc_ref))):
        t = _sigmoid(g_ref[...]) * _dot(y_ref[...], wb_ref[n])
        acc = t if acc is None else acc + t
    o_ref[...] = acc.astype(o_ref.dtype)


def _merge(ya, yb, yc, wb_bf, layer, proj, tm, tn):
    m = ya.shape[0]
    gate0 = COL_GATE * BRANCH_W // tn
    per = D_MODEL // tn
    y_spec = pl.BlockSpec((tm, BRANCH_W), lambda i, j: (i, 0))
    g_specs = [pl.BlockSpec((tm, tn), functools.partial(lambda i, j, n: (i, gate0 + n * per + j), n=n))
               for n in range(N_BRANCH)]
    return pl.pallas_call(
        _merge_kernel,
        grid=(m // tm, D_MODEL // tn),
        in_specs=[y_spec, y_spec, y_spec,
                  pl.BlockSpec((None, N_BRANCH, BRANCH_W, tn), lambda i, j: (layer, 0, 0, j))] + g_specs,
        out_specs=pl.BlockSpec((tm, tn), lambda i, j: (i, j)),
        out_shape=jax.ShapeDtypeStruct((m, D_MODEL), bf16),
        compiler_params=_params(("parallel", "parallel")),
    )(ya, yb, yc, wb_bf, proj, proj, proj)


def _outproj_kernel(m_ref, w_ref, x_ref, o_ref):
    o_ref[...] = x_ref[...] + _dot(m_ref[...], w_ref[...])


def _out_proj(mm, w_bf, layer, x, tm, tn):
    m = x.shape[0]
    return pl.pallas_call(
        _outproj_kernel,
        grid=(m // tm, D_MODEL // tn),
        in_specs=[pl.BlockSpec((tm, D_MODEL), lambda i, j: (i, 0)),
                  pl.BlockSpec((None, D_MODEL, tn), lambda i, j: (layer, 0, j)),
                  pl.BlockSpec((tm, tn), lambda i, j: (i, j))],
        out_specs=pl.BlockSpec((tm, tn), lambda i, j: (i, j)),
        out_shape=jax.ShapeDtypeStruct((m, D_MODEL), f32),
        compiler_params=_params(("parallel", "parallel")),
    )(mm, w_bf, x)


def _conv_kernel(ch_ref, cb_ref, cc_ref, cz_ref, w_ref, ya_ref, nc_ref, carry_ref, *, tt):
    @pl.when(pl.program_id(1) == 0)
    def _():
        carry_ref[...] = jnp.zeros_like(carry_ref)

    u = cc_ref[0] * ch_ref[0]
    prev2 = carry_ref[0:1, :]
    prev1 = carry_ref[1:2, :]
    row = lax.broadcasted_iota(jnp.int32, u.shape, 0)
    u1 = jnp.where(row == 0, prev1, pltpu.roll(u, 1, axis=0))
    u2 = jnp.where(row == 0, prev2, jnp.where(row == 1, prev1, pltpu.roll(u, 2, axis=0)))
    y = w_ref[0:1, :] * u2 + w_ref[1:2, :] * u1 + w_ref[2:3, :] * u
    ya_ref[0] = (cb_ref[0] * y * _silu(cz_ref[0])).astype(ya_ref.dtype)
    tail = u[tt - 2:tt, :]
    carry_ref[0:2, :] = tail
    nc_ref[0] = tail


def _conv_prompt(proj, conv_w, tt):
    b, t, _ = proj.shape
    specs = [pl.BlockSpec((1, tt, BRANCH_W), functools.partial(lambda i, j, c: (i, j, c), c=COL_CONV + c))
             for c in range(4)]
    return pl.pallas_call(
        functools.partial(_conv_kernel, tt=tt),
        grid=(b, t // tt),
        in_specs=specs + [pl.BlockSpec((CONV_WIDTH, BRANCH_W), lambda i, j: (0, 0))],
        out_specs=(pl.BlockSpec((1, tt, BRANCH_W), lambda i, j: (i, j, 0)),
                   pl.BlockSpec((1, CONV_WIDTH - 1, BRANCH_W), lambda i, j: (i, 0, 0))),
        out_shape=(jax.ShapeDtypeStruct((b, t, BRANCH_W), bf16),
                   jax.ShapeDtypeStruct((b, CONV_WIDTH - 1, BRANCH_W), f32)),
        scratch_shapes=[pltpu.VMEM((8, BRANCH_W), f32)],
        compiler_params=_params(("parallel", "arbitrary")),
    )(proj, proj, proj, proj, conv_w)


def _rec_gates(rq, rf, lb):
    e = jnp.exp(-jnp.abs(rf))
    r = 1.0 / (1.0 + e)
    er = e * r
    pos = rf >= 0
    sig = jnp.where(pos, r, er)
    nsig = jnp.where(pos, er, r)
    g = jnp.log(lb + (1.0 - lb) * sig)
    k = (1.0 - lb) * nsig
    return _silu(rq), k, g


def _hgrn_kernel(rq_ref, rf_ref, ri_ref, rg_ref, lb_ref, onw_ref, yb_ref, st_ref, st_scr, *, tt, hp, unroll):
    c_len = REC_CHUNK

    @pl.when(pl.program_id(2) == 0)
    def _():
        st_scr[...] = jnp.zeros_like(st_scr)

    onw = onw_ref[...]
    ri = lax.broadcasted_iota(jnp.int32, (c_len, c_len), 0)
    ci = lax.broadcasted_iota(jnp.int32, (c_len, c_len), 1)
    tri = (ci <= ri).astype(bf16)
    ones_kc = jnp.ones((HEAD_W, c_len), bf16)
    rowk = lax.broadcasted_iota(jnp.int32, (c_len, HEAD_W), 0)
    level_masks = []
    s = REC_SUB
    while s < c_len:
        shift = int(math.log2(2 * s))
        level_masks.append((s, ((((ri ^ ci) >> shift) | ((ri & s) ^ s) | (ci & s)) == 0)))
        s *= 2
    sub_shift = int(math.log2(REC_SUB))
    diag_masks = [((((ri - ci) ^ d) | ((ri >> sub_shift) ^ (ci >> sub_shift))) == 0) for d in range(REC_SUB)]
    valid_rows = [(rowk & (REC_SUB - 1)) >= d for d in range(REC_SUB)]

    def head_chunk(r0, hh):
        rows = pl.ds(r0, c_len)
        lanes = slice(hh * HEAD_W, (hh + 1) * HEAD_W)
        q, k, g = _rec_gates(rq_ref[0, rows, lanes], rf_ref[0, rows, lanes], lb_ref[:, lanes])
        v_bf = ri_ref[0, rows, lanes].astype(bf16)
        g_hi = g.astype(bf16)
        rem = g - g_hi.astype(f32)
        g_mid = rem.astype(bf16)
        g_lo = (rem - g_mid.astype(f32)).astype(bf16)
        b = _dot(tri, g_lo) + _dot(tri, g_mid) + _dot(tri, g_hi)
        b_last = b[c_len - 1:c_len, :]

        a = jnp.zeros((c_len, c_len), f32)
        for s, mask in level_masks:
            refs = [jnp.broadcast_to(b[m + s - 1:m + s, :], (2 * s, HEAD_W)) for m in range(0, c_len, 2 * s)]
            z = jnp.exp(-jnp.abs(b - jnp.concatenate(refs, axis=0)))
            a = a + jnp.where(mask, _dot_nt((q * z).astype(bf16), (k * z).astype(bf16)), 0.0)
        for d in range(REC_SUB):
            if d == 0:
                p = q * k
            else:
                ex = jnp.where(valid_rows[d], b - pltpu.roll(b, d, axis=0), 0.0)
                p = q * pltpu.roll(k, d, axis=0) * jnp.exp(ex)
            a_d = _dot(p.astype(bf16), ones_kc)
            a = a + jnp.where(diag_masks[d], a_d, 0.0)

        st = st_scr[hh]
        o = _dot(a.astype(bf16), v_bf) + _dot_nt((q * jnp.exp(b)).astype(bf16), st.astype(bf16))
        ms = jnp.mean(o * o, axis=-1, keepdims=True)
        y = o * lax.rsqrt(ms + EPS) * onw * _silu(rg_ref[0, rows, lanes])
        yb_ref[0, rows, lanes] = y.astype(yb_ref.dtype)
        k_dec = (k * jnp.exp(b_last - b)).astype(bf16)
        st_scr[hh] = st * jnp.exp(b_last) + _dot_tn(v_bf, k_dec)

    def chunk(c, carry):
        r0 = pl.multiple_of(c * c_len, c_len)
        for hh in range(hp):
            head_chunk(r0, hh)
        return carry

    lax.fori_loop(0, tt // c_len, chunk, 0, unroll=unroll)

    @pl.when(pl.program_id(2) == pl.num_programs(2) - 1)
    def _():
        st_ref[0] = st_scr[...]


def _hgrn_prompt(proj, lb, out_norm, tt, hp=REC_HEADS_PER_STEP, unroll=REC_UNROLL):
    b, t, _ = proj.shape
    base = COL_REC * N_HEADS // hp
    specs = [pl.BlockSpec((1, tt, hp * HEAD_W),
                          functools.partial(lambda i, h, j, c: (i, j, c + h), c=base + c * N_HEADS // hp))
             for c in range(4)]
    return pl.pallas_call(
        functools.partial(_hgrn_kernel, tt=tt, hp=hp, unroll=unroll),
        grid=(b, N_HEADS // hp, t // tt),
        in_specs=specs + [pl.BlockSpec((1, hp * HEAD_W), lambda i, h, j: (0, h)),
                          pl.BlockSpec((1, HEAD_W), lambda i, h, j: (0, 0))],
        out_specs=(pl.BlockSpec((1, tt, hp * HEAD_W), lambda i, h, j: (i, j, h)),
                   pl.BlockSpec((1, hp, HEAD_W, HEAD_W), lambda i, h, j: (i, h, 0, 0))),
        out_shape=(jax.ShapeDtypeStruct((b, t, BRANCH_W), bf16),
                   jax.ShapeDtypeStruct((b, N_HEADS, HEAD_W, HEAD_W), f32)),
        scratch_shapes=[pltpu.VMEM((hp, HEAD_W, HEAD_W), f32)],
        compiler_params=_params(("parallel", "parallel", "arbitrary")),
    )(proj, proj, proj, proj, lb.reshape(1, BRANCH_W), out_norm.reshape(1, HEAD_W))


def _rope_tables(pos):
    half = ROT_DIM // 2
    inv_freq = ROPE_THETA ** (-jnp.arange(half, dtype=f32) * 2.0 / ROT_DIM)
    ang = pos.astype(f32)[:, None] * inv_freq[None, :]
    cos, sin = jnp.cos(ang), jnp.sin(ang)
    t = pos.shape[0]
    one = jnp.ones((t, ATT_HD - ROT_DIM), f32)
    zero = jnp.zeros((t, ATT_HD - ROT_DIM), f32)
    zh = jnp.zeros((t, half), f32)
    cos_t = jnp.concatenate([cos, cos, one], axis=-1)
    sa = jnp.concatenate([-sin, zh, zero], axis=-1)
    sb = jnp.concatenate([zh, sin, zero], axis=-1)
    return tuple(jnp.concatenate([x, x], axis=-1) for x in (cos_t, sa, sb))


def _component_mean_sq(x, grp):
    sq = x * x
    hi = sq.astype(bf16)
    lo = (sq - hi.astype(f32)).astype(bf16)
    return (_dot(lo, grp) + _dot(hi, grp)) * (1.0 / ATT_HD)


def _qk_norm_rope(x, w, grp, cos_t, sa, sb):
    xn = x * lax.rsqrt(_component_mean_sq(x, grp) + EPS) * w
    half = ROT_DIM // 2
    return xn * cos_t + pltpu.roll(xn, HEAD_W - half, axis=1) * sa + pltpu.roll(xn, half, axis=1) * sb


def _group_matrix():
    r = lax.broadcasted_iota(jnp.int32, (HEAD_W, HEAD_W), 0)
    c = lax.broadcasted_iota(jnp.int32, (HEAD_W, HEAD_W), 1)
    return ((r < ATT_HD) == (c < ATT_HD)).astype(bf16)


def _qkprep_kernel(aq_ref, ak_ref, av_ref, qn_ref, kn_ref, cos_ref, sa_ref, sb_ref,
                   kf_ref, kb_ref, vb_ref, qz_ref):
    grp = _group_matrix()
    cos_t, sa, sb = cos_ref[...], sa_ref[...], sb_ref[...]
    lane = lax.broadcasted_iota(jnp.int32, cos_t.shape, 1)
    first = lane < ATT_HD
    scale = Q_SCALE
    for h in range(N_HEADS):
        sl = slice(h * HEAD_W, (h + 1) * HEAD_W)
        kk = _qk_norm_rope(ak_ref[0, :, sl], kn_ref[...], grp, cos_t, sa, sb)
        kf_ref[0, :, sl] = kk
        kb_ref[0, :, sl] = kk.astype(bf16)
        qq = _qk_norm_rope(aq_ref[0, :, sl], qn_ref[...], grp, cos_t, sa, sb) * scale
        qz_ref[0, 0, :, sl] = jnp.where(first, qq, 0.0).astype(bf16)
        qz_ref[0, 1, :, sl] = jnp.where(first, 0.0, qq).astype(bf16)
    vb_ref[0] = av_ref[0].astype(bf16)


def _qk_prep(proj, q_norm, k_norm, tables, tt):
    b, t, _ = proj.shape
    specs = [pl.BlockSpec((1, tt, BRANCH_W), functools.partial(lambda i, j, c: (i, j, c), c=COL_ATT + c))
             for c in range(3)]
    w_spec = pl.BlockSpec((1, HEAD_W), lambda i, j: (0, 0))
    t_spec = pl.BlockSpec((tt, HEAD_W), lambda i, j: (j, 0))
    o_spec = pl.BlockSpec((1, tt, BRANCH_W), lambda i, j: (i, j, 0))
    tile2 = lambda w: jnp.concatenate([w, w]).reshape(1, HEAD_W)
    return pl.pallas_call(
        _qkprep_kernel,
        grid=(b, t // tt),
        in_specs=specs + [w_spec, w_spec, t_spec, t_spec, t_spec],
        out_specs=(o_spec, o_spec, o_spec,
                   pl.BlockSpec((1, 2, tt, BRANCH_W), lambda i, j: (i, 0, j, 0))),
        out_shape=(jax.ShapeDtypeStruct((b, t, BRANCH_W), f32),
                   jax.ShapeDtypeStruct((b, t, BRANCH_W), bf16),
                   jax.ShapeDtypeStruct((b, t, BRANCH_W), bf16),
                   jax.ShapeDtypeStruct((b, 2, t, BRANCH_W), bf16)),
        compiler_params=_params(("parallel", "parallel")),
    )(proj, proj, proj, tile2(q_norm), tile2(k_norm), *tables)


def _subln_gate(o, subln, coef, az):
    ms = jnp.mean(o * o, axis=-1, keepdims=True)
    return o * lax.rsqrt(ms + EPS) * subln * coef * _silu(az)


def _flash_kernel(qz_ref, k_ref, v_ref, az_ref, lam_ref, sub_ref, o_ref, m_scr, acc_scr,
                  *, tq, coef, hpf):
    rows = min(FLASH_ROWS, tq)
    qi = pl.program_id(2)
    m_scr[...] = jnp.full_like(m_scr, MASK_VALUE)
    acc_scr[...] = jnp.zeros_like(acc_scr)

    def update(ki, on_diagonal):
        keys = pl.ds(pl.multiple_of(ki * tq, tq), tq)
        for hh in range(hpf):
            lanes = slice(hh * HEAD_W, (hh + 1) * HEAD_W)
            k = k_ref[0, keys, lanes]
            v_ones = jnp.concatenate([v_ref[0, keys, lanes], jnp.ones((tq, HEAD_W), bf16)], axis=1)
            for r0 in range(0, 2 * tq, rows):
                comp, q0 = divmod(r0, tq)
                sl = slice(r0, r0 + rows)
                s = _dot_nt(qz_ref[0, comp, q0:q0 + rows, lanes], k)
                if on_diagonal:
                    r = lax.broadcasted_iota(jnp.int32, s.shape, 0)
                    c = lax.broadcasted_iota(jnp.int32, s.shape, 1)
                    s = jnp.where(c <= r + q0, s, MASK_VALUE)
                m_old = m_scr[hh, sl]
                m_new = jnp.maximum(m_old, jnp.max(s, axis=-1, keepdims=True))
                alpha = jnp.exp2(m_old - m_new)
                p = jnp.exp2((s - jnp.concatenate([m_new] * (tq // HEAD_W), axis=1)).astype(bf16))
                acc_scr[hh, sl] = jnp.concatenate([alpha, alpha], axis=1) * acc_scr[hh, sl] + _dot(p, v_ones)
                m_scr[hh, sl] = m_new

    def below_diagonal(ki, carry):
        update(ki, False)
        return carry

    lax.fori_loop(0, qi, below_diagonal, 0)
    update(qi, True)
    for hh in range(hpf):
        lanes = slice(hh * HEAD_W, (hh + 1) * HEAD_W)
        on = acc_scr[hh, :, 0:HEAD_W] / acc_scr[hh, :, HEAD_W:2 * HEAD_W]
        o = on[0:tq] - lam_ref[...] * on[tq:2 * tq]
        o_ref[0, :, lanes] = _subln_gate(o, sub_ref[...], coef, az_ref[0, :, lanes]).astype(o_ref.dtype)


def _flash_prompt(qz, k_bf, v_bf, proj, lam_row, subln, coef, tq, hpf=FLASH_HEADS_PER_STEP):
    b, t, _ = k_bf.shape
    n = t // tq
    az_col = (COL_ATT + 3) * N_HEADS // hpf
    w = hpf * HEAD_W
    return pl.pallas_call(
        functools.partial(_flash_kernel, tq=tq, coef=coef, hpf=hpf),
        grid=(b, N_HEADS // hpf, n),
        in_specs=[pl.BlockSpec((1, 2, tq, w), lambda i, h, q: (i, 0, q, h)),
                  pl.BlockSpec((1, t, w), lambda i, h, q: (i, 0, h)),
                  pl.BlockSpec((1, t, w), lambda i, h, q: (i, 0, h)),
                  pl.BlockSpec((1, tq, w), lambda i, h, q: (i, q, az_col + h)),
                  pl.BlockSpec((1, HEAD_W), lambda i, h, q: (0, 0)),
                  pl.BlockSpec((1, HEAD_W), lambda i, h, q: (0, 0))],
        out_specs=pl.BlockSpec((1, tq, w), lambda i, h, q: (i, q, h)),
        out_shape=jax.ShapeDtypeStruct((b, t, BRANCH_W), bf16),
        scratch_shapes=[pltpu.VMEM((hpf, 2 * tq, HEAD_W), f32),
                        pltpu.VMEM((hpf, 2 * tq, 2 * HEAD_W), f32)],
        compiler_params=_params(("parallel", "parallel", "parallel")),
    )(qz, k_bf, v_bf, proj, lam_row, subln.reshape(1, HEAD_W))


def _lane_to_sublane(row):
    r = lax.broadcasted_iota(jnp.int32, (HEAD_W, HEAD_W), 0)
    c = lax.broadcasted_iota(jnp.int32, (HEAD_W, HEAD_W), 1)
    return jnp.sum(jnp.where(r == c, jnp.broadcast_to(row, (HEAD_W, HEAD_W)), 0.0), axis=1, keepdims=True)


def _sample_mix_kernel(p_ref, cbuf_ref, st_ref, cw_ref, lb_ref, onw_ref, qn_ref, kn_ref,
                       cos_ref, sa_ref, sb_ref,
                       ya_ref, yb_ref, nc_ref, nst_ref, kf_ref, qm_ref):
    blk = lambda c: p_ref[0, c * N_HEADS:(c + 1) * N_HEADS, :]
    u = blk(COL_CONV + 2) * blk(COL_CONV)
    y = cw_ref[0] * cbuf_ref[0, 0] + cw_ref[1] * cbuf_ref[0, 1] + cw_ref[2] * u
    ya_ref[0] = (blk(COL_CONV + 1) * y * _silu(blk(COL_CONV + 3))).astype(ya_ref.dtype)
    nc_ref[0, 0] = cbuf_ref[0, 1]
    nc_ref[0, 1] = u
    q, k, g = _rec_gates(blk(COL_REC), blk(COL_REC + 1), lb_ref[...])
    v = blk(COL_REC + 2)
    dec = jnp.exp(g)
    outs = []
    for h in range(N_HEADS):
        st_new = st_ref[0, h] * dec[h:h + 1, :] + _lane_to_sublane(v[h:h + 1, :]) * k[h:h + 1, :]
        nst_ref[0, h] = st_new
        qh = jnp.broadcast_to(q[h:h + 1, :], (8, HEAD_W)).astype(bf16)
        outs.append(_dot_nt(qh, st_new.astype(bf16))[0:1, :])
    o = jnp.concatenate(outs, axis=0)
    ms = jnp.mean(o * o, axis=-1, keepdims=True)
    yb_ref[0] = (o * lax.rsqrt(ms + EPS) * onw_ref[...] * _silu(blk(COL_REC + 3))).astype(yb_ref.dtype)
    grp = _group_matrix()
    cos_t, sa, sb = cos_ref[...], sa_ref[...], sb_ref[...]
    kf_ref[0] = _qk_norm_rope(blk(COL_ATT + 1), kn_ref[...], grp, cos_t, sa, sb)
    qq = _qk_norm_rope(blk(COL_ATT), qn_ref[...], grp, cos_t, sa, sb) * Q_SCALE
    r = lax.broadcasted_iota(jnp.int32, (2 * N_HEADS, HEAD_W), 0)
    lane = lax.broadcasted_iota(jnp.int32, (2 * N_HEADS, HEAD_W), 1)
    for h in range(N_HEADS):
        qh = jnp.broadcast_to(qq[h:h + 1, :], (2 * N_HEADS, HEAD_W))
        keep = r == 2 * h + jnp.where(lane < ATT_HD, 0, 1)
        qm_ref[0, :, h * HEAD_W:(h + 1) * HEAD_W] = jnp.where(keep, qh, 0.0).astype(bf16)


def _sample_mix(proj_s, conv_state, st_t, conv_w, lb, out_norm, q_norm, k_norm, tables):
    nb = proj_s.shape[0]
    rows = N_IN // HEAD_W
    tile2 = lambda w: jnp.concatenate([w, w]).reshape(1, HEAD_W)
    vec = pl.BlockSpec((1, HEAD_W), lambda i: (0, 0))
    hw = (N_HEADS, HEAD_W)
    out_row = lambda dt: (pl.BlockSpec((1,) + hw, lambda i: (i, 0, 0)), jax.ShapeDtypeStruct((nb,) + hw, dt))
    outs = [out_row(f32), out_row(f32),
            (pl.BlockSpec((1, CONV_WIDTH - 1) + hw, lambda i: (i, 0, 0, 0)),
             jax.ShapeDtypeStruct((nb, CONV_WIDTH - 1) + hw, f32)),
            (pl.BlockSpec((1, N_HEADS, HEAD_W, HEAD_W), lambda i: (i, 0, 0, 0)),
             jax.ShapeDtypeStruct((nb, N_HEADS, HEAD_W, HEAD_W), f32)),
            out_row(f32),
            (pl.BlockSpec((1, 2 * N_HEADS, BRANCH_W), lambda i: (i, 0, 0)),
             jax.ShapeDtypeStruct((nb, 2 * N_HEADS, BRANCH_W), bf16))]
    return pl.pallas_call(
        _sample_mix_kernel,
        grid=(nb,),
        in_specs=[pl.BlockSpec((1, rows, HEAD_W), lambda i: (i, 0, 0)),
                  pl.BlockSpec((1, CONV_WIDTH - 1) + hw, lambda i: (i, 0, 0, 0)),
                  pl.BlockSpec((1, N_HEADS, HEAD_W, HEAD_W), lambda i: (i, 0, 0, 0)),
                  pl.BlockSpec((CONV_WIDTH,) + hw, lambda i: (0, 0, 0)),
                  pl.BlockSpec(hw, lambda i: (0, 0)),
                  vec, vec, vec, vec, vec, vec],
        out_specs=tuple(o[0] for o in outs),
        out_shape=tuple(o[1] for o in outs),
        compiler_params=_params(("parallel",)),
    )(proj_s.reshape(nb, rows, HEAD_W), conv_state.reshape((nb, CONV_WIDTH - 1) + hw), st_t,
      conv_w.reshape((CONV_WIDTH,) + hw), lb.reshape(hw), out_norm.reshape(1, HEAD_W),
      tile2(q_norm), tile2(k_norm), *tables)


def _decode_kernel(pt_ref, qm_ref, kn_ref, vn_ref, az_ref, lam_ref, sub_ref, *rest, coef):
    g = PAGES_PER_STEP
    k_refs, v_refs = rest[:g], rest[g:2 * g]
    o_ref, m_scr, l_scr, acc_scr, fin_scr, exp_scr = rest[2 * g:]
    step = pl.program_id(1)
    q = qm_ref[0]
    n_rows = PAGE_SIZE * N_HEADS
    head_shift = int(math.log2(N_HEADS))

    @pl.when(step == 0)
    def _():
        s0 = jnp.sum(q.astype(f32) * kn_ref[0], axis=-1, keepdims=True)
        m_scr[...] = jnp.broadcast_to(s0, m_scr.shape)
        l_scr[...] = jnp.ones_like(l_scr)
        acc_scr[...] = vn_ref[0]
        pos = lax.broadcasted_iota(jnp.int32, (PAGE_SIZE, n_rows), 0)
        col = lax.broadcasted_iota(jnp.int32, (PAGE_SIZE, n_rows), 1)
        exp_scr[...] = jnp.where((col >> head_shift) == pos, 1.0, 0.0).astype(bf16)

    s = jnp.concatenate([_dot(q, k_refs[i][0, 0].astype(bf16)) for i in range(g)], axis=-1)
    m_old = m_scr[...]
    m_new = jnp.maximum(m_old, jnp.max(s, axis=-1, keepdims=True))
    alpha = jnp.exp2(m_old - m_new)
    p = jnp.exp2(s - jnp.concatenate([m_new] * g, axis=1))
    l_scr[...] = alpha * l_scr[...] + jnp.sum(p, axis=-1, keepdims=True)
    n_q = 2 * N_HEADS
    p_bf = p.astype(bf16)
    stacked = jnp.concatenate([p_bf[:, i * PAGE_SIZE:(i + 1) * PAGE_SIZE] for i in range(g)], axis=0)
    r = lax.broadcasted_iota(jnp.int32, (g * n_q, n_rows), 0)
    col = lax.broadcasted_iota(jnp.int32, (g * n_q, n_rows), 1)
    own_head = (col & (N_HEADS - 1)) == ((r & (n_q - 1)) >> 1)
    spread = jnp.where(own_head, _dot(stacked, exp_scr[...]), 0.0).astype(bf16)
    pv = None
    for i in range(g):
        t = _dot(spread[i * n_q:(i + 1) * n_q], v_refs[i][0, 0].astype(bf16))
        pv = t if pv is None else pv + t
    acc_scr[...] = alpha * acc_scr[...] + pv
    m_scr[...] = m_new

    @pl.when(step == pl.num_programs(1) - 1)
    def _():
        row = lax.broadcasted_iota(jnp.int32, acc_scr.shape, 0)
        fin_scr[...] = acc_scr[...] / l_scr[...] * jnp.where((row & 1) == 0, 1.0, -lam_ref[...])
        o = fin_scr[pl.ds(0, N_HEADS, stride=2), :] + fin_scr[pl.ds(1, N_HEADS, stride=2), :]
        o_ref[0] = _subln_gate(o, sub_ref[...], coef, az_ref[0]).astype(o_ref.dtype)


def _decode_attention(page_table, qmat, k_new, v_new, az, lam_row, subln, cache_k, cache_v, layer, coef):
    nb, n_pages = page_table.shape
    g = PAGES_PER_STEP
    per_batch = lambda rows, width: pl.BlockSpec((1, rows, width), lambda b, s, pt: (b, 0, 0))
    vec = pl.BlockSpec((1, HEAD_W), lambda b, s, pt: (0, 0))
    page = lambda rows, width: [
        pl.BlockSpec((1, 1, rows, width),
                     functools.partial(lambda b, s, pt, i: (layer, pt[b, s * g + i], 0, 0), i=i))
        for i in range(g)]
    acc = pltpu.VMEM((2 * N_HEADS, HEAD_W), f32)
    grid_spec = pltpu.PrefetchScalarGridSpec(
        num_scalar_prefetch=1,
        grid=(nb, n_pages // g),
        in_specs=[per_batch(2 * N_HEADS, BRANCH_W), per_batch(1, BRANCH_W), per_batch(2 * N_HEADS, HEAD_W),
                  per_batch(N_HEADS, HEAD_W), vec, vec]
        + page(BRANCH_W, PAGE_SIZE) + page(PAGE_SIZE * N_HEADS, HEAD_W),
        out_specs=per_batch(N_HEADS, HEAD_W),
        scratch_shapes=[acc, acc, acc, acc, pltpu.VMEM((PAGE_SIZE, PAGE_SIZE * N_HEADS), bf16)])
    return pl.pallas_call(
        functools.partial(_decode_kernel, coef=coef),
        grid_spec=grid_spec,
        out_shape=jax.ShapeDtypeStruct((nb, N_HEADS, HEAD_W), f32),
        compiler_params=_params(("parallel", "arbitrary")),
    )(page_table, qmat, k_new, v_new, az, lam_row, subln.reshape(1, HEAD_W),
      *([cache_k] * g), *([cache_v] * g))


def kernel(x_prompt, x_sample, cache_k, cache_v, state_conv, state_rec, page_table, norm_w, w_in, conv_w,
           rec_lb_logits, rec_out_norm, q_norm, k_norm, lambda_qk, attn_subln, w_branch, w_out):
    nbp, seq, _ = x_prompt.shape
    nbs = x_sample.shape[0]
    n_pool = cache_k.shape[1]
    lb_all, lam_all = _layer_params(rec_lb_logits, lambda_qk)
    w_branch_bf, w_out_bf = w_branch.astype(bf16), w_out.astype(bf16)
    ck = jnp.transpose(cache_k, (0, 1, 3, 4, 5, 2)).reshape(DEPTH, n_pool, BRANCH_W, PAGE_SIZE)
    cv = cache_v.reshape(DEPTH, n_pool, PAGE_SIZE * N_HEADS, HEAD_W)
    tab_p = _rope_tables(jnp.arange(seq, dtype=jnp.int32))
    tab_s = _rope_tables(jnp.full((1,), page_table.shape[1] * PAGE_SIZE, jnp.int32))

    mp = nbp * seq
    tm = min(1024, mp)
    tt = min(512, seq)
    y_p = x_prompt.reshape(mp, D_MODEL)
    ms = -(-nbs // BF16_SUBLANES) * BF16_SUBLANES
    pad_rows = lambda a: jnp.pad(a, ((0, ms - nbs), (0, 0)))
    y_s = pad_rows(x_sample.reshape(nbs, D_MODEL))
    outs = [[] for _ in range(8)]
    for l in range(DEPTH):
        coef = 1.0 - (0.8 - 0.6 * math.exp(-0.3 * l))
        lam_row = lam_all[l:l + 1]

        proj, proj_pad = _in_proj(_rmsnorm_bf16(y_p, norm_w[l], tm), _rmsnorm_bf16(y_s, norm_w[l], ms),
                                  w_in, l, tm, 1024)
        proj3 = proj.reshape(nbp, seq, N_IN)
        ya, c_p = _conv_prompt(proj3, conv_w[l], tt)
        yb, r_p = _hgrn_prompt(proj3, lb_all[l], rec_out_norm[l], tt)
        k_p, k_bf, v_bf, qz = _qk_prep(proj3, q_norm[l], k_norm[l], tab_p, min(256, seq))
        yc = _flash_prompt(qz, k_bf, v_bf, proj3, lam_row, attn_subln[l], coef, tt)
        mm = _merge(ya.reshape(mp, BRANCH_W), yb.reshape(mp, BRANCH_W), yc.reshape(mp, BRANCH_W),
                    w_branch_bf, l, proj, tm, 512)
        y_p = _out_proj(mm, w_out_bf, l, y_p, tm, 1024)
        v_p = proj3[:, :, (COL_ATT + 2) * BRANCH_W:(COL_ATT + 3) * BRANCH_W]

        proj_s = proj_pad[:nbs]
        ya_s, yb_s, c_s, st_s, k_s, qmat = _sample_mix(
            proj_s, state_conv[l], jnp.swapaxes(state_rec[l], -1, -2), conv_w[l], lb_all[l],
            rec_out_norm[l], q_norm[l], k_norm[l], tab_s)
        v_s = proj_s[:, (COL_ATT + 2) * BRANCH_W:(COL_ATT + 3) * BRANCH_W]
        az_s = proj_s[:, (COL_ATT + 3) * BRANCH_W:(COL_ATT + 4) * BRANCH_W]
        v_rows = jnp.repeat(v_s.reshape(nbs, N_HEADS, HEAD_W), 2, axis=1)
        yc_s = _decode_attention(page_table, qmat, k_s.reshape(nbs, 1, BRANCH_W), v_rows,
                                 az_s.reshape(nbs, N_HEADS, HEAD_W), lam_row, attn_subln[l], ck, cv, l, coef)
        mm_s = _merge(pad_rows(ya_s.reshape(nbs, BRANCH_W)).astype(bf16),
                      pad_rows(yb_s.reshape(nbs, BRANCH_W)).astype(bf16),
                      pad_rows(yc_s.reshape(nbs, BRANCH_W)).astype(bf16),
                      w_branch_bf, l, proj_pad, ms, 512)
        y_s = _out_proj(mm_s, w_out_bf, l, y_s, ms, 1024)

        for lst, val in zip(outs, (
                k_p.reshape(nbp, seq, N_HEADS, 2, ATT_HD), v_p.reshape(nbp, seq, N_HEADS, HEAD_W),
                k_s.reshape(nbs, 1, N_HEADS, 2, ATT_HD), v_s.reshape(nbs, 1, N_HEADS, HEAD_W),
                c_p, c_s.reshape(nbs, CONV_WIDTH - 1, BRANCH_W),
                jnp.swapaxes(r_p, -1, -2), jnp.swapaxes(st_s, -1, -2))):
            lst.append(val)

    return (y_p.reshape(nbp, seq, D_MODEL), y_s[:nbs].reshape(nbs, 1, D_MODEL)) + tuple(jnp.stack(o, axis=0) for o in outs)
```

```python
import functools
import math

import jax
import jax.numpy as jnp
import numpy as np
from jax import lax
from jax.experimental import pallas as pl
from jax.experimental.pallas import tpu as pltpu

D_MODEL = 2048
DEPTH = 4
PAST_LEN = 16384
PAGE_SIZE = 128
BRANCH_W = D_MODEL // 2
CONV_WIDTH = 3
N_HEADS = 8
HEAD_W = BRANCH_W // N_HEADS
ATT_HD = HEAD_W // 2
ROT_DIM = ATT_HD // 4
ROPE_THETA = 500000.0
N_BRANCH = 3
EPS = 1e-6
MASK_VALUE = -1e30
N_IN = 12 * BRANCH_W + N_BRANCH * D_MODEL
COL_CONV, COL_REC, COL_ATT, COL_GATE = 0, 4, 8, 12

V7X_VMEM_BYTES = 64 * 1024 * 1024
BF16_SUBLANES = 16
VMEM_LIMIT = 56 * 1024 * 1024
REC_CHUNK = 128
REC_SUB = 2
REC_HEADS_PER_STEP = 8
REC_UNROLL = 1
Q_SCALE = ATT_HD ** -0.5 * math.log2(math.e)
FLASH_ROWS = 256
FLASH_HEADS_PER_STEP = 8
PAGES_PER_STEP = 8

f32 = jnp.float32
bf16 = jnp.bfloat16
PROJ_DTYPE = bf16


def _params(sem, vmem=VMEM_LIMIT):
    return pltpu.CompilerParams(dimension_semantics=sem, vmem_limit_bytes=vmem)


def _sigmoid(x):
    return 1.0 / (1.0 + jnp.exp(-x))


def _silu(x):
    return x * _sigmoid(x)


def _dot(a, b):
    return jnp.dot(a, b, preferred_element_type=f32)


def _dot_nt(a, b):
    return lax.dot_general(a, b, (((1,), (1,)), ((), ())), preferred_element_type=f32)


def _dot_tn(a, b):
    return lax.dot_general(a, b, (((0,), (0,)), ((), ())), preferred_element_type=f32)


def _param_kernel(logit_ref, lq_ref, lb_ref, lam_ref):
    x = logit_ref[...]
    m = jnp.max(x, axis=0, keepdims=True)
    e = jnp.exp(x - m)
    soft = e / jnp.sum(e, axis=0, keepdims=True)
    run = soft[0:1]
    rows = [run - soft[0:1]]
    for l in range(1, DEPTH):
        run = run + soft[l:l + 1]
        rows.append(run - soft[0:1])
    lb_ref[...] = jnp.concatenate(rows, axis=0)
    lams = []
    for l in range(DEPTH):
        lq = lq_ref[l]
        a = jnp.sum(lq[0:1] * lq[1:2], axis=-1, keepdims=True)
        b = jnp.sum(lq[2:3] * lq[3:4], axis=-1, keepdims=True)
        lam_init = 0.8 - 0.6 * math.exp(-0.3 * l)
        lams.append(jnp.broadcast_to(jnp.exp(a) - jnp.exp(b) + lam_init, (1, HEAD_W)))
    lam_ref[...] = jnp.concatenate(lams, axis=0)


def _layer_params(rec_lb_logits, lambda_qk):
    return pl.pallas_call(
        _param_kernel,
        out_shape=(jax.ShapeDtypeStruct(rec_lb_logits.shape, f32),
                   jax.ShapeDtypeStruct((DEPTH, HEAD_W), f32)),
    )(rec_lb_logits.astype(f32), lambda_qk.astype(f32))


def _rmsnorm_kernel(x_ref, nw_ref, h_ref):
    x = x_ref[...]
    ms = jnp.mean(x * x, axis=-1, keepdims=True)
    h_ref[...] = (x * lax.rsqrt(ms + EPS) * nw_ref[...]).astype(h_ref.dtype)


def _rmsnorm_bf16(x, norm_w, tm):
    m, d = x.shape
    return pl.pallas_call(
        _rmsnorm_kernel,
        grid=(m // tm,),
        in_specs=[pl.BlockSpec((tm, d), lambda i: (i, 0)), pl.BlockSpec((1, d), lambda i: (0, 0))],
        out_specs=pl.BlockSpec((tm, d), lambda i: (i, 0)),
        out_shape=jax.ShapeDtypeStruct((m, d), bf16),
        compiler_params=_params(("parallel",)),
    )(x, norm_w.reshape(1, d))


def _inproj_kernel(hp_ref, hs_ref, w_ref, op_ref, os_ref, wb_scr):
    @pl.when(pl.program_id(1) == 0)
    def _():
        wb_scr[...] = w_ref[...].astype(bf16)
        os_ref[...] = _dot(hs_ref[...], wb_scr[...])

    op_ref[...] = _dot(hp_ref[...], wb_scr[...]).astype(op_ref.dtype)


def _in_proj(h_p, h_s, w, layer, tm, tn):
    m, d = h_p.shape
    ms = h_s.shape[0]
    n = w.shape[2]
    return pl.pallas_call(
        _inproj_kernel,
        grid=(n // tn, m // tm),
        in_specs=[pl.BlockSpec((tm, d), lambda j, i: (i, 0)),
                  pl.BlockSpec((ms, d), lambda j, i: (0, 0)),
                  pl.BlockSpec((None, d, tn), lambda j, i: (layer, 0, j))],
        out_specs=(pl.BlockSpec((tm, tn), lambda j, i: (i, j)),
                   pl.BlockSpec((ms, tn), lambda j, i: (0, j))),
        out_shape=(jax.ShapeDtypeStruct((m, n), PROJ_DTYPE), jax.ShapeDtypeStruct((ms, n), f32)),
        scratch_shapes=[pltpu.VMEM((d, tn), bf16)],
        compiler_params=_params(("parallel", "arbitrary")),
    )(h_p, h_s, w)


def _merge_kernel(ya_ref, yb_ref, yc_ref, wb_ref, ga_ref, gb_ref, gc_ref, o_ref):
    acc = None
    for n, (y_ref, g_ref) in enumerate(((ya_ref, ga_ref), (yb_ref, gb_ref), (yc_ref, gc_ref))):
        t = _sigmoid(g_ref[...].astype(f32)) * _dot(y_ref[...], wb_ref[n])
        acc = t if acc is None else acc + t
    o_ref[...] = acc.astype(o_ref.dtype)


def _merge(ya, yb, yc, wb_bf, layer, proj, tm, tn):
    m = ya.shape[0]
    gate0 = COL_GATE * BRANCH_W // tn
    per = D_MODEL // tn
    y_spec = pl.BlockSpec((tm, BRANCH_W), lambda i, j: (i, 0))
    g_specs = [pl.BlockSpec((tm, tn), functools.partial(lambda i, j, n: (i, gate0 + n * per + j), n=n))
               for n in range(N_BRANCH)]
    return pl.pallas_call(
        _merge_kernel,
        grid=(m // tm, D_MODEL // tn),
        in_specs=[y_spec, y_spec, y_spec,
                  pl.BlockSpec((None, N_BRANCH, BRANCH_W, tn), lambda i, j: (layer, 0, 0, j))] + g_specs,
        out_specs=pl.BlockSpec((tm, tn), lambda i, j: (i, j)),
        out_shape=jax.ShapeDtypeStruct((m, D_MODEL), bf16),
        compiler_params=_params(("parallel", "parallel")),
    )(ya, yb, yc, wb_bf, proj, proj, proj)


def _outproj_kernel(m_ref, w_ref, x_ref, o_ref):
    o_ref[...] = x_ref[...] + _dot(m_ref[...], w_ref[...])


def _out_proj(mm, w_bf, layer, x, tm, tn):
    m = x.shape[0]
    return pl.pallas_call(
        _outproj_kernel,
        grid=(m // tm, D_MODEL // tn),
        in_specs=[pl.BlockSpec((tm, D_MODEL), lambda i, j: (i, 0)),
                  pl.BlockSpec((None, D_MODEL, tn), lambda i, j: (layer, 0, j)),
                  pl.BlockSpec((tm, tn), lambda i, j: (i, j))],
        out_specs=pl.BlockSpec((tm, tn), lambda i, j: (i, j)),
        out_shape=jax.ShapeDtypeStruct((m, D_MODEL), f32),
        compiler_params=_params(("parallel", "parallel")),
    )(mm, w_bf, x)


def _conv_kernel(ch_ref, cb_ref, cc_ref, cz_ref, w_ref, ya_ref, nc_ref, carry_ref, *, tt):
    @pl.when(pl.program_id(1) == 0)
    def _():
        carry_ref[...] = jnp.zeros_like(carry_ref)

    u = cc_ref[0].astype(f32) * ch_ref[0].astype(f32)
    prev2 = carry_ref[0:1, :]
    prev1 = carry_ref[1:2, :]
    row = lax.broadcasted_iota(jnp.int32, u.shape, 0)
    u1 = jnp.where(row == 0, prev1, pltpu.roll(u, 1, axis=0))
    u2 = jnp.where(row == 0, prev2, jnp.where(row == 1, prev1, pltpu.roll(u, 2, axis=0)))
    y = w_ref[0:1, :] * u2 + w_ref[1:2, :] * u1 + w_ref[2:3, :] * u
    ya_ref[0] = (cb_ref[0].astype(f32) * y * _silu(cz_ref[0].astype(f32))).astype(ya_ref.dtype)
    tail = u[tt - 2:tt, :]
    carry_ref[0:2, :] = tail
    nc_ref[0] = tail


def _conv_prompt(proj, conv_w, tt):
    b, t, _ = proj.shape
    specs = [pl.BlockSpec((1, tt, BRANCH_W), functools.partial(lambda i, j, c: (i, j, c), c=COL_CONV + c))
             for c in range(4)]
    return pl.pallas_call(
        functools.partial(_conv_kernel, tt=tt),
        grid=(b, t // tt),
        in_specs=specs + [pl.BlockSpec((CONV_WIDTH, BRANCH_W), lambda i, j: (0, 0))],
        out_specs=(pl.BlockSpec((1, tt, BRANCH_W), lambda i, j: (i, j, 0)),
                   pl.BlockSpec((1, CONV_WIDTH - 1, BRANCH_W), lambda i, j: (i, 0, 0))),
        out_shape=(jax.ShapeDtypeStruct((b, t, BRANCH_W), bf16),
                   jax.ShapeDtypeStruct((b, CONV_WIDTH - 1, BRANCH_W), f32)),
        scratch_shapes=[pltpu.VMEM((8, BRANCH_W), f32)],
        compiler_params=_params(("parallel", "arbitrary")),
    )(proj, proj, proj, proj, conv_w)


def _rec_gates(rq, rf, lb):
    e = jnp.exp(-jnp.abs(rf))
    r = 1.0 / (1.0 + e)
    er = e * r
    pos = rf >= 0
    sig = jnp.where(pos, r, er)
    nsig = jnp.where(pos, er, r)
    g = jnp.log(lb + (1.0 - lb) * sig)
    k = (1.0 - lb) * nsig
    return _silu(rq), k, g


def _hgrn_kernel(rq_ref, rf_ref, ri_ref, rg_ref, lb_ref, onw_ref, yb_ref, st_ref, st_scr, *, tt, hp, unroll):
    c_len = REC_CHUNK

    @pl.when(pl.program_id(2) == 0)
    def _():
        st_scr[...] = jnp.zeros_like(st_scr)

    onw = onw_ref[...]
    ri = lax.broadcasted_iota(jnp.int32, (c_len, c_len), 0)
    ci = lax.broadcasted_iota(jnp.int32, (c_len, c_len), 1)
    tri = (ci <= ri).astype(bf16)
    ones_kc = jnp.ones((HEAD_W, c_len), bf16)
    rowk = lax.broadcasted_iota(jnp.int32, (c_len, HEAD_W), 0)
    level_masks = []
    s = REC_SUB
    while s < c_len:
        shift = int(math.log2(2 * s))
        level_masks.append((s, ((((ri ^ ci) >> shift) | ((ri & s) ^ s) | (ci & s)) == 0)))
        s *= 2
    sub_shift = int(math.log2(REC_SUB))
    diag_masks = [((((ri - ci) ^ d) | ((ri >> sub_shift) ^ (ci >> sub_shift))) == 0) for d in range(REC_SUB)]
    valid_rows = [(rowk & (REC_SUB - 1)) >= d for d in range(REC_SUB)]

    def head_chunk(r0, hh):
        rows = pl.ds(r0, c_len)
        lanes = slice(hh * HEAD_W, (hh + 1) * HEAD_W)
        q, k, g = _rec_gates(rq_ref[0, rows, lanes].astype(f32), rf_ref[0, rows, lanes].astype(f32),
                             lb_ref[:, lanes])
        v_bf = ri_ref[0, rows, lanes].astype(bf16)
        g_hi = g.astype(bf16)
        rem = g - g_hi.astype(f32)
        g_mid = rem.astype(bf16)
        g_lo = (rem - g_mid.astype(f32)).astype(bf16)
        b = _dot(tri, g_lo) + _dot(tri, g_mid) + _dot(tri, g_hi)
        b_last = b[c_len - 1:c_len, :]

        a = jnp.zeros((c_len, c_len), f32)
        for s, mask in level_masks:
            refs = [jnp.broadcast_to(b[m + s - 1:m + s, :], (2 * s, HEAD_W)) for m in range(0, c_len, 2 * s)]
            z = jnp.exp(-jnp.abs(b - jnp.concatenate(refs, axis=0)))
            a = jnp.where(mask, _dot_nt((q * z).astype(bf16), (k * z).astype(bf16)), a)
        for d in range(REC_SUB):
            if d == 0:
                p = q * k
            else:
                ex = jnp.where(valid_rows[d], b - pltpu.roll(b, d, axis=0), 0.0)
                p = q * pltpu.roll(k, d, axis=0) * jnp.exp(ex)
            a_d = _dot(p.astype(bf16), ones_kc)
            a = jnp.where(diag_masks[d], a_d, a)

        st = st_scr[hh]
        o = _dot(a.astype(bf16), v_bf) + _dot_nt((q * jnp.exp(b)).astype(bf16), st.astype(bf16))
        ms = jnp.mean(o * o, axis=-1, keepdims=True)
        y = o * lax.rsqrt(ms + EPS) * onw * _silu(rg_ref[0, rows, lanes].astype(f32))
        yb_ref[0, rows, lanes] = y.astype(yb_ref.dtype)
        k_dec = (k * jnp.exp(b_last - b)).astype(bf16)
        st_scr[hh] = st * jnp.exp(b_last) + _dot_tn(v_bf, k_dec)

    def chunk(c, carry):
        r0 = pl.multiple_of(c * c_len, c_len)
        for hh in range(hp):
            head_chunk(r0, hh)
        return carry

    lax.fori_loop(0, tt // c_len, chunk, 0, unroll=unroll)

    @pl.when(pl.program_id(2) == pl.num_programs(2) - 1)
    def _():
        st_ref[0] = st_scr[...]


def _hgrn_prompt(proj, lb, out_norm, tt, hp=REC_HEADS_PER_STEP, unroll=REC_UNROLL):
    b, t, _ = proj.shape
    base = COL_REC * N_HEADS // hp
    specs = [pl.BlockSpec((1, tt, hp * HEAD_W),
                          functools.partial(lambda i, h, j, c: (i, j, c + h), c=base + c * N_HEADS // hp))
             for c in range(4)]
    return pl.pallas_call(
        functools.partial(_hgrn_kernel, tt=tt, hp=hp, unroll=unroll),
        grid=(b, N_HEADS // hp, t // tt),
        in_specs=specs + [pl.BlockSpec((1, hp * HEAD_W), lambda i, h, j: (0, h)),
                          pl.BlockSpec((1, HEAD_W), lambda i, h, j: (0, 0))],
        out_specs=(pl.BlockSpec((1, tt, hp * HEAD_W), lambda i, h, j: (i, j, h)),
                   pl.BlockSpec((1, hp, HEAD_W, HEAD_W), lambda i, h, j: (i, h, 0, 0))),
        out_shape=(jax.ShapeDtypeStruct((b, t, BRANCH_W), bf16),
                   jax.ShapeDtypeStruct((b, N_HEADS, HEAD_W, HEAD_W), f32)),
        scratch_shapes=[pltpu.VMEM((hp, HEAD_W, HEAD_W), f32)],
        compiler_params=_params(("parallel", "parallel", "arbitrary")),
    )(proj, proj, proj, proj, lb.reshape(1, BRANCH_W), out_norm.reshape(1, HEAD_W))


def _rope_tables(pos):
    half = ROT_DIM // 2
    inv_freq = ROPE_THETA ** (-jnp.arange(half, dtype=f32) * 2.0 / ROT_DIM)
    ang = pos.astype(f32)[:, None] * inv_freq[None, :]
    cos, sin = jnp.cos(ang), jnp.sin(ang)
    t = pos.shape[0]
    one = jnp.ones((t, ATT_HD - ROT_DIM), f32)
    zero = jnp.zeros((t, ATT_HD - ROT_DIM), f32)
    zh = jnp.zeros((t, half), f32)
    cos_t = jnp.concatenate([cos, cos, one], axis=-1)
    sa = jnp.concatenate([-sin, zh, zero], axis=-1)
    sb = jnp.concatenate([zh, sin, zero], axis=-1)
    return tuple(jnp.concatenate([x, x], axis=-1) for x in (cos_t, sa, sb))


def _component_mean_sq(x, grp):
    sq = x * x
    hi = sq.astype(bf16)
    lo = (sq - hi.astype(f32)).astype(bf16)
    return (_dot(lo, grp) + _dot(hi, grp)) * (1.0 / ATT_HD)


def _qk_norm_rope(x, w, grp, cos_t, sa, sb):
    xn = x * lax.rsqrt(_component_mean_sq(x, grp) + EPS) * w
    half = ROT_DIM // 2
    return xn * cos_t + pltpu.roll(xn, HEAD_W - half, axis=1) * sa + pltpu.roll(xn, half, axis=1) * sb


def _group_matrix():
    r = lax.broadcasted_iota(jnp.int32, (HEAD_W, HEAD_W), 0)
    c = lax.broadcasted_iota(jnp.int32, (HEAD_W, HEAD_W), 1)
    return ((r < ATT_HD) == (c < ATT_HD)).astype(bf16)


def _qkprep_kernel(aq_ref, ak_ref, qn_ref, kn_ref, cos_ref, sa_ref, sb_ref, kf_ref, kb_ref, qz_ref):
    grp = _group_matrix()
    cos_t, sa, sb = cos_ref[...], sa_ref[...], sb_ref[...]
    lane = lax.broadcasted_iota(jnp.int32, cos_t.shape, 1)
    first = lane < ATT_HD
    scale = Q_SCALE
    for h in range(N_HEADS):
        sl = slice(h * HEAD_W, (h + 1) * HEAD_W)
        kk = _qk_norm_rope(ak_ref[0, :, sl].astype(f32), kn_ref[...], grp, cos_t, sa, sb)
        kf_ref[0, :, sl] = kk
        kb_ref[0, :, sl] = kk.astype(bf16)
        qq = _qk_norm_rope(aq_ref[0, :, sl].astype(f32), qn_ref[...], grp, cos_t, sa, sb) * scale
        qz_ref[0, 0, :, sl] = jnp.where(first, qq, 0.0).astype(bf16)
        qz_ref[0, 1, :, sl] = jnp.where(first, 0.0, qq).astype(bf16)


def _qk_prep(proj, q_norm, k_norm, tables, tt):
    b, t, _ = proj.shape
    specs = [pl.BlockSpec((1, tt, BRANCH_W), functools.partial(lambda i, j, c: (i, j, c), c=COL_ATT + c))
             for c in range(2)]
    w_spec = pl.BlockSpec((1, HEAD_W), lambda i, j: (0, 0))
    t_spec = pl.BlockSpec((tt, HEAD_W), lambda i, j: (j, 0))
    o_spec = pl.BlockSpec((1, tt, BRANCH_W), lambda i, j: (i, j, 0))
    tile2 = lambda w: jnp.concatenate([w, w]).reshape(1, HEAD_W)
    return pl.pallas_call(
        _qkprep_kernel,
        grid=(b, t // tt),
        in_specs=specs + [w_spec, w_spec, t_spec, t_spec, t_spec],
        out_specs=(o_spec, o_spec,
                   pl.BlockSpec((1, 2, tt, BRANCH_W), lambda i, j: (i, 0, j, 0))),
        out_shape=(jax.ShapeDtypeStruct((b, t, BRANCH_W), f32),
                   jax.ShapeDtypeStruct((b, t, BRANCH_W), bf16),
                   jax.ShapeDtypeStruct((b, 2, t, BRANCH_W), bf16)),
        compiler_params=_params(("parallel", "parallel")),
    )(proj, proj, tile2(q_norm), tile2(k_norm), *tables)


def _subln_gate(o, subln, coef, az):
    ms = jnp.mean(o * o, axis=-1, keepdims=True)
    return o * lax.rsqrt(ms + EPS) * subln * coef * _silu(az)


def _flash_kernel(qz_ref, k_ref, v_ref, az_ref, lam_ref, sub_ref, o_ref, m_scr, acc_scr,
                  *, tq, coef, hpf):
    rows = min(FLASH_ROWS, tq)
    qi = pl.program_id(2)
    m_scr[...] = jnp.full_like(m_scr, MASK_VALUE)
    acc_scr[...] = jnp.zeros_like(acc_scr)

    def update(ki, on_diagonal):
        keys = pl.ds(pl.multiple_of(ki * tq, tq), tq)
        for hh in range(hpf):
            lanes = slice(hh * HEAD_W, (hh + 1) * HEAD_W)
            k = k_ref[0, keys, lanes]
            v_ones = jnp.concatenate([v_ref[0, keys, lanes], jnp.ones((tq, HEAD_W), bf16)], axis=1)
            for r0 in range(0, 2 * tq, rows):
                comp, q0 = divmod(r0, tq)
                sl = slice(r0, r0 + rows)
                s = _dot_nt(qz_ref[0, comp, q0:q0 + rows, lanes], k)
                if on_diagonal:
                    r = lax.broadcasted_iota(jnp.int32, s.shape, 0)
                    c = lax.broadcasted_iota(jnp.int32, s.shape, 1)
                    s = jnp.where(c <= r + q0, s, MASK_VALUE)
                m_old = m_scr[hh, sl]
                m_new = jnp.maximum(m_old, jnp.max(s, axis=-1, keepdims=True))
                alpha = jnp.exp2(m_old - m_new)
                p = jnp.exp2((s - jnp.concatenate([m_new] * (tq // HEAD_W), axis=1)).astype(bf16))
                acc_scr[hh, sl] = jnp.concatenate([alpha, alpha], axis=1) * acc_scr[hh, sl] + _dot(p, v_ones)
                m_scr[hh, sl] = m_new

    def below_diagonal(ki, carry):
        update(ki, False)
        return carry

    lax.fori_loop(0, qi, below_diagonal, 0)
    update(qi, True)
    for hh in range(hpf):
        lanes = slice(hh * HEAD_W, (hh + 1) * HEAD_W)
        on = acc_scr[hh, :, 0:HEAD_W] / acc_scr[hh, :, HEAD_W:2 * HEAD_W]
        o = on[0:tq] - lam_ref[...] * on[tq:2 * tq]
        o_ref[0, :, lanes] = _subln_gate(o, sub_ref[...], coef,
                                         az_ref[0, :, lanes].astype(f32)).astype(o_ref.dtype)


def _flash_prompt(qz, k_bf, proj, lam_row, subln, coef, tq, hpf=FLASH_HEADS_PER_STEP):
    b, t, _ = k_bf.shape
    n = t // tq
    v_col = (COL_ATT + 2) * N_HEADS // hpf
    az_col = (COL_ATT + 3) * N_HEADS // hpf
    w = hpf * HEAD_W
    return pl.pallas_call(
        functools.partial(_flash_kernel, tq=tq, coef=coef, hpf=hpf),
        grid=(b, N_HEADS // hpf, n),
        in_specs=[pl.BlockSpec((1, 2, tq, w), lambda i, h, q: (i, 0, q, h)),
                  pl.BlockSpec((1, t, w), lambda i, h, q: (i, 0, h)),
                  pl.BlockSpec((1, t, w), lambda i, h, q: (i, 0, v_col + h)),
                  pl.BlockSpec((1, tq, w), lambda i, h, q: (i, q, az_col + h)),
                  pl.BlockSpec((1, HEAD_W), lambda i, h, q: (0, 0)),
                  pl.BlockSpec((1, HEAD_W), lambda i, h, q: (0, 0))],
        out_specs=pl.BlockSpec((1, tq, w), lambda i, h, q: (i, q, h)),
        out_shape=jax.ShapeDtypeStruct((b, t, BRANCH_W), bf16),
        scratch_shapes=[pltpu.VMEM((hpf, 2 * tq, HEAD_W), f32),
                        pltpu.VMEM((hpf, 2 * tq, 2 * HEAD_W), f32)],
        compiler_params=_params(("parallel", "parallel", "parallel")),
    )(qz, k_bf, proj, proj, lam_row, subln.reshape(1, HEAD_W))


def _lane_to_sublane(row):
    r = lax.broadcasted_iota(jnp.int32, (HEAD_W, HEAD_W), 0)
    c = lax.broadcasted_iota(jnp.int32, (HEAD_W, HEAD_W), 1)
    return jnp.sum(jnp.where(r == c, jnp.broadcast_to(row, (HEAD_W, HEAD_W)), 0.0), axis=1, keepdims=True)


def _sample_mix_kernel(p_ref, cbuf_ref, st_ref, cw_ref, lb_ref, onw_ref, qn_ref, kn_ref,
                       cos_ref, sa_ref, sb_ref,
                       ya_ref, yb_ref, nc_ref, nst_ref, kf_ref, qm_ref):
    blk = lambda c: p_ref[0, c * N_HEADS:(c + 1) * N_HEADS, :]
    u = blk(COL_CONV + 2) * blk(COL_CONV)
    y = cw_ref[0] * cbuf_ref[0, 0] + cw_ref[1] * cbuf_ref[0, 1] + cw_ref[2] * u
    ya_ref[0] = (blk(COL_CONV + 1) * y * _silu(blk(COL_CONV + 3))).astype(ya_ref.dtype)
    nc_ref[0, 0] = cbuf_ref[0, 1]
    nc_ref[0, 1] = u
    q, k, g = _rec_gates(blk(COL_REC), blk(COL_REC + 1), lb_ref[...])
    v = blk(COL_REC + 2)
    dec = jnp.exp(g)
    outs = []
    for h in range(N_HEADS):
        st_new = st_ref[0, h] * dec[h:h + 1, :] + _lane_to_sublane(v[h:h + 1, :]) * k[h:h + 1, :]
        nst_ref[0, h] = st_new
        qh = jnp.broadcast_to(q[h:h + 1, :], (8, HEAD_W)).astype(bf16)
        outs.append(_dot_nt(qh, st_new.astype(bf16))[0:1, :])
    o = jnp.concatenate(outs, axis=0)
    ms = jnp.mean(o * o, axis=-1, keepdims=True)
    yb_ref[0] = (o * lax.rsqrt(ms + EPS) * onw_ref[...] * _silu(blk(COL_REC + 3))).astype(yb_ref.dtype)
    grp = _group_matrix()
    cos_t, sa, sb = cos_ref[...], sa_ref[...], sb_ref[...]
    kf_ref[0] = _qk_norm_rope(blk(COL_ATT + 1), kn_ref[...], grp, cos_t, sa, sb)
    qq = _qk_norm_rope(blk(COL_ATT), qn_ref[...], grp, cos_t, sa, sb) * Q_SCALE
    r = lax.broadcasted_iota(jnp.int32, (2 * N_HEADS, HEAD_W), 0)
    lane = lax.broadcasted_iota(jnp.int32, (2 * N_HEADS, HEAD_W), 1)
    for h in range(N_HEADS):
        qh = jnp.broadcast_to(qq[h:h + 1, :], (2 * N_HEADS, HEAD_W))
        keep = r == 2 * h + jnp.where(lane < ATT_HD, 0, 1)
        qm_ref[0, :, h * HEAD_W:(h + 1) * HEAD_W] = jnp.where(keep, qh, 0.0).astype(bf16)


def _sample_mix(proj_s, conv_state, st_t, conv_w, lb, out_norm, q_norm, k_norm, tables):
    nb = proj_s.shape[0]
    rows = N_IN // HEAD_W
    tile2 = lambda w: jnp.concatenate([w, w]).reshape(1, HEAD_W)
    vec = pl.BlockSpec((1, HEAD_W), lambda i: (0, 0))
    hw = (N_HEADS, HEAD_W)
    out_row = lambda dt: (pl.BlockSpec((1,) + hw, lambda i: (i, 0, 0)), jax.ShapeDtypeStruct((nb,) + hw, dt))
    outs = [out_row(f32), out_row(f32),
            (pl.BlockSpec((1, CONV_WIDTH - 1) + hw, lambda i: (i, 0, 0, 0)),
             jax.ShapeDtypeStruct((nb, CONV_WIDTH - 1) + hw, f32)),
            (pl.BlockSpec((1, N_HEADS, HEAD_W, HEAD_W), lambda i: (i, 0, 0, 0)),
             jax.ShapeDtypeStruct((nb, N_HEADS, HEAD_W, HEAD_W), f32)),
            out_row(f32),
            (pl.BlockSpec((1, 2 * N_HEADS, BRANCH_W), lambda i: (i, 0, 0)),
             jax.ShapeDtypeStruct((nb, 2 * N_HEADS, BRANCH_W), bf16))]
    return pl.pallas_call(
        _sample_mix_kernel,
        grid=(nb,),
        in_specs=[pl.BlockSpec((1, rows, HEAD_W), lambda i: (i, 0, 0)),
                  pl.BlockSpec((1, CONV_WIDTH - 1) + hw, lambda i: (i, 0, 0, 0)),
                  pl.BlockSpec((1, N_HEADS, HEAD_W, HEAD_W), lambda i: (i, 0, 0, 0)),
                  pl.BlockSpec((CONV_WIDTH,) + hw, lambda i: (0, 0, 0)),
                  pl.BlockSpec(hw, lambda i: (0, 0)),
                  vec, vec, vec, vec, vec, vec],
        out_specs=tuple(o[0] for o in outs),
        out_shape=tuple(o[1] for o in outs),
        compiler_params=_params(("parallel",)),
    )(proj_s.reshape(nb, rows, HEAD_W), conv_state.reshape((nb, CONV_WIDTH - 1) + hw), st_t,
      conv_w.reshape((CONV_WIDTH,) + hw), lb.reshape(hw), out_norm.reshape(1, HEAD_W),
      tile2(q_norm), tile2(k_norm), *tables)


def _decode_kernel(pt_ref, qm_ref, kn_ref, vn_ref, az_ref, lam_ref, sub_ref, *rest, coef):
    g = PAGES_PER_STEP
    k_refs, v_refs = rest[:g], rest[g:2 * g]
    o_ref, m_scr, l_scr, acc_scr, fin_scr, exp_scr = rest[2 * g:]
    step = pl.program_id(1)
    q = qm_ref[0]
    n_rows = PAGE_SIZE * N_HEADS
    head_shift = int(math.log2(N_HEADS))

    @pl.when(step == 0)
    def _():
        s0 = jnp.sum(q.astype(f32) * kn_ref[0], axis=-1, keepdims=True)
        m_scr[...] = jnp.broadcast_to(s0, m_scr.shape)
        l_scr[...] = jnp.ones_like(l_scr)
        acc_scr[...] = vn_ref[0]
        pos = lax.broadcasted_iota(jnp.int32, (PAGE_SIZE, n_rows), 0)
        col = lax.broadcasted_iota(jnp.int32, (PAGE_SIZE, n_rows), 1)
        exp_scr[...] = jnp.where((col >> head_shift) == pos, 1.0, 0.0).astype(bf16)

    s = jnp.concatenate([_dot(q, k_refs[i][0, 0].astype(bf16)) for i in range(g)], axis=-1)
    m_old = m_scr[...]
    m_new = jnp.maximum(m_old, jnp.max(s, axis=-1, keepdims=True))
    alpha = jnp.exp2(m_old - m_new)
    p = jnp.exp2(s - jnp.concatenate([m_new] * g, axis=1))
    l_scr[...] = alpha * l_scr[...] + jnp.sum(p, axis=-1, keepdims=True)
    n_q = 2 * N_HEADS
    p_bf = p.astype(bf16)
    stacked = jnp.concatenate([p_bf[:, i * PAGE_SIZE:(i + 1) * PAGE_SIZE] for i in range(g)], axis=0)
    r = lax.broadcasted_iota(jnp.int32, (g * n_q, n_rows), 0)
    col = lax.broadcasted_iota(jnp.int32, (g * n_q, n_rows), 1)
    own_head = (col & (N_HEADS - 1)) == ((r & (n_q - 1)) >> 1)
    spread = jnp.where(own_head, _dot(stacked, exp_scr[...]), 0.0).astype(bf16)
    pv = None
    for i in range(g):
        t = _dot(spread[i * n_q:(i + 1) * n_q], v_refs[i][0, 0].astype(bf16))
        pv = t if pv is None else pv + t
    acc_scr[...] = alpha * acc_scr[...] + pv
    m_scr[...] = m_new

    @pl.when(step == pl.num_programs(1) - 1)
    def _():
        row = lax.broadcasted_iota(jnp.int32, acc_scr.shape, 0)
        fin_scr[...] = acc_scr[...] / l_scr[...] * jnp.where((row & 1) == 0, 1.0, -lam_ref[...])
        o = fin_scr[pl.ds(0, N_HEADS, stride=2), :] + fin_scr[pl.ds(1, N_HEADS, stride=2), :]
        o_ref[0] = _subln_gate(o, sub_ref[...], coef, az_ref[0]).astype(o_ref.dtype)


def _decode_attention(page_table, qmat, k_new, v_new, az, lam_row, subln, cache_k, cache_v, layer, coef):
    nb, n_pages = page_table.shape
    g = PAGES_PER_STEP
    per_batch = lambda rows, width: pl.BlockSpec((1, rows, width), lambda b, s, pt: (b, 0, 0))
    vec = pl.BlockSpec((1, HEAD_W), lambda b, s, pt: (0, 0))
    page = lambda rows, width: [
        pl.BlockSpec((1, 1, rows, width),
                     functools.partial(lambda b, s, pt, i: (layer, pt[b, s * g + i], 0, 0), i=i))
        for i in range(g)]
    acc = pltpu.VMEM((2 * N_HEADS, HEAD_W), f32)
    grid_spec = pltpu.PrefetchScalarGridSpec(
        num_scalar_prefetch=1,
        grid=(nb, n_pages // g),
        in_specs=[per_batch(2 * N_HEADS, BRANCH_W), per_batch(1, BRANCH_W), per_batch(2 * N_HEADS, HEAD_W),
                  per_batch(N_HEADS, HEAD_W), vec, vec]
        + page(BRANCH_W, PAGE_SIZE) + page(PAGE_SIZE * N_HEADS, HEAD_W),
        out_specs=per_batch(N_HEADS, HEAD_W),
        scratch_shapes=[acc, acc, acc, acc, pltpu.VMEM((PAGE_SIZE, PAGE_SIZE * N_HEADS), bf16)])
    return pl.pallas_call(
        functools.partial(_decode_kernel, coef=coef),
        grid_spec=grid_spec,
        out_shape=jax.ShapeDtypeStruct((nb, N_HEADS, HEAD_W), f32),
        compiler_params=_params(("parallel", "arbitrary")),
    )(page_table, qmat, k_new, v_new, az, lam_row, subln.reshape(1, HEAD_W),
      *([cache_k] * g), *([cache_v] * g))


def kernel(x_prompt, x_sample, cache_k, cache_v, state_conv, state_rec, page_table, norm_w, w_in, conv_w,
           rec_lb_logits, rec_out_norm, q_norm, k_norm, lambda_qk, attn_subln, w_branch, w_out):
    nbp, seq, _ = x_prompt.shape
    nbs = x_sample.shape[0]
    n_pool = cache_k.shape[1]
    lb_all, lam_all = _layer_params(rec_lb_logits, lambda_qk)
    w_branch_bf, w_out_bf = w_branch.astype(bf16), w_out.astype(bf16)
    ck = jnp.transpose(cache_k, (0, 1, 3, 4, 5, 2)).reshape(DEPTH, n_pool, BRANCH_W, PAGE_SIZE)
    cv = cache_v.reshape(DEPTH, n_pool, PAGE_SIZE * N_HEADS, HEAD_W)
    tab_p = _rope_tables(jnp.arange(seq, dtype=jnp.int32))
    tab_s = _rope_tables(jnp.full((1,), page_table.shape[1] * PAGE_SIZE, jnp.int32))

    mp = nbp * seq
    tm = min(1024, mp)
    tt = min(512, seq)
    y_p = x_prompt.reshape(mp, D_MODEL)
    ms = -(-nbs // BF16_SUBLANES) * BF16_SUBLANES
    pad_rows = lambda a: jnp.pad(a, ((0, ms - nbs), (0, 0)))
    y_s = pad_rows(x_sample.reshape(nbs, D_MODEL))
    outs = [[] for _ in range(8)]
    for l in range(DEPTH):
        coef = 1.0 - (0.8 - 0.6 * math.exp(-0.3 * l))
        lam_row = lam_all[l:l + 1]

        proj, proj_pad = _in_proj(_rmsnorm_bf16(y_p, norm_w[l], tm), _rmsnorm_bf16(y_s, norm_w[l], ms),
                                  w_in, l, tm, 1024)
        proj3 = proj.reshape(nbp, seq, N_IN)
        ya, c_p = _conv_prompt(proj3, conv_w[l], tt)
        yb, r_p = _hgrn_prompt(proj3, lb_all[l], rec_out_norm[l], tt)
        k_p, k_bf, qz = _qk_prep(proj3, q_norm[l], k_norm[l], tab_p, min(256, seq))
        yc = _flash_prompt(qz, k_bf, proj3, lam_row, attn_subln[l], coef, tt)
        mm = _merge(ya.reshape(mp, BRANCH_W), yb.reshape(mp, BRANCH_W), yc.reshape(mp, BRANCH_W),
                    w_branch_bf, l, proj, tm, 512)
        y_p = _out_proj(mm, w_out_bf, l, y_p, tm, 1024)
        v_p = proj3[:, :, (COL_ATT + 2) * BRANCH_W:(COL_ATT + 3) * BRANCH_W].astype(f32)

        proj_s = proj_pad[:nbs]
        ya_s, yb_s, c_s, st_s, k_s, qmat = _sample_mix(
            proj_s, state_conv[l], jnp.swapaxes(state_rec[l], -1, -2), conv_w[l], lb_all[l],
            rec_out_norm[l], q_norm[l], k_norm[l], tab_s)
        v_s = proj_s[:, (COL_ATT + 2) * BRANCH_W:(COL_ATT + 3) * BRANCH_W]
        az_s = proj_s[:, (COL_ATT + 3) * BRANCH_W:(COL_ATT + 4) * BRANCH_W]
        v_rows = jnp.repeat(v_s.reshape(nbs, N_HEADS, HEAD_W), 2, axis=1)
        yc_s = _decode_attention(page_table, qmat, k_s.reshape(nbs, 1, BRANCH_W), v_rows,
                                 az_s.reshape(nbs, N_HEADS, HEAD_W), lam_row, attn_subln[l], ck, cv, l, coef)
        mm_s = _merge(pad_rows(ya_s.reshape(nbs, BRANCH_W)).astype(bf16),
                      pad_rows(yb_s.reshape(nbs, BRANCH_W)).astype(bf16),
                      pad_rows(yc_s.reshape(nbs, BRANCH_W)).astype(bf16),
                      w_branch_bf, l, proj_pad, ms, 512)
        y_s = _out_proj(mm_s, w_out_bf, l, y_s, ms, 1024)

        for lst, val in zip(outs, (
                k_p.reshape(nbp, seq, N_HEADS, 2, ATT_HD), v_p.reshape(nbp, seq, N_HEADS, HEAD_W),
                k_s.reshape(nbs, 1, N_HEADS, 2, ATT_HD), v_s.reshape(nbs, 1, N_HEADS, HEAD_W),
                c_p, c_s.reshape(nbs, CONV_WIDTH - 1, BRANCH_W),
                jnp.swapaxes(r_p, -1, -2), jnp.swapaxes(st_s, -1, -2))):
            lst.append(val)

    return (y_p.reshape(nbp, seq, D_MODEL), y_s[:nbs].reshape(nbs, 1, D_MODEL)) + tuple(jnp.stack(o, axis=0) for o in outs)
```

```python
import functools
import math

import jax
import jax.numpy as jnp
import numpy as np
from jax import lax
from jax.experimental import pallas as pl
from jax.experimental.pallas import tpu as pltpu

D_MODEL = 2048
DEPTH = 4
PAST_LEN = 16384
PAGE_SIZE = 128
BRANCH_W = D_MODEL // 2
CONV_WIDTH = 3
N_HEADS = 8
HEAD_W = BRANCH_W // N_HEADS
ATT_HD = HEAD_W // 2
ROT_DIM = ATT_HD // 4
ROPE_THETA = 500000.0
N_BRANCH = 3
EPS = 1e-6
MASK_VALUE = -1e30
N_IN = 12 * BRANCH_W + N_BRANCH * D_MODEL
COL_CONV, COL_REC, COL_ATT, COL_GATE = 0, 4, 8, 12

V7X_VMEM_BYTES = 64 * 1024 * 1024
BF16_SUBLANES = 16
VMEM_LIMIT = 56 * 1024 * 1024
REC_CHUNK = 128
REC_SUB = 2
REC_HEADS_PER_STEP = 8
REC_UNROLL = 1
Q_SCALE = ATT_HD ** -0.5 * math.log2(math.e)
FLASH_ROWS = 256
FLASH_HEADS_PER_STEP = 8
PAGES_PER_STEP = 8

f32 = jnp.float32
bf16 = jnp.bfloat16
PROJ_DTYPE = bf16


def _params(sem, vmem=VMEM_LIMIT):
    return pltpu.CompilerParams(dimension_semantics=sem, vmem_limit_bytes=vmem)


def _sigmoid(x):
    return 1.0 / (1.0 + jnp.exp(-x))


def _sigmoid_tanh(x):
    return 0.5 * jnp.tanh(0.5 * x) + 0.5


def _silu(x):
    return x * _sigmoid(x)


def _dot(a, b):
    return jnp.dot(a, b, preferred_element_type=f32)


def _dot_nt(a, b):
    return lax.dot_general(a, b, (((1,), (1,)), ((), ())), preferred_element_type=f32)


def _dot_tn(a, b):
    return lax.dot_general(a, b, (((0,), (0,)), ((), ())), preferred_element_type=f32)


def _param_kernel(logit_ref, lq_ref, lb_ref, lam_ref):
    x = logit_ref[...]
    m = jnp.max(x, axis=0, keepdims=True)
    e = jnp.exp(x - m)
    soft = e / jnp.sum(e, axis=0, keepdims=True)
    run = soft[0:1]
    rows = [run - soft[0:1]]
    for l in range(1, DEPTH):
        run = run + soft[l:l + 1]
        rows.append(run - soft[0:1])
    lb_ref[...] = jnp.concatenate(rows, axis=0)
    lams = []
    for l in range(DEPTH):
        lq = lq_ref[l]
        a = jnp.sum(lq[0:1] * lq[1:2], axis=-1, keepdims=True)
        b = jnp.sum(lq[2:3] * lq[3:4], axis=-1, keepdims=True)
        lam_init = 0.8 - 0.6 * math.exp(-0.3 * l)
        lams.append(jnp.broadcast_to(jnp.exp(a) - jnp.exp(b) + lam_init, (1, HEAD_W)))
    lam_ref[...] = jnp.concatenate(lams, axis=0)


def _layer_params(rec_lb_logits, lambda_qk):
    return pl.pallas_call(
        _param_kernel,
        out_shape=(jax.ShapeDtypeStruct(rec_lb_logits.shape, f32),
                   jax.ShapeDtypeStruct((DEPTH, HEAD_W), f32)),
    )(rec_lb_logits.astype(f32), lambda_qk.astype(f32))


def _rmsnorm_kernel(x_ref, nw_ref, h_ref):
    x = x_ref[...]
    ms = jnp.mean(x * x, axis=-1, keepdims=True)
    h_ref[...] = (x * lax.rsqrt(ms + EPS) * nw_ref[...]).astype(h_ref.dtype)


def _rmsnorm_bf16(x, norm_w, tm):
    m, d = x.shape
    return pl.pallas_call(
        _rmsnorm_kernel,
        grid=(m // tm,),
        in_specs=[pl.BlockSpec((tm, d), lambda i: (i, 0)), pl.BlockSpec((1, d), lambda i: (0, 0))],
        out_specs=pl.BlockSpec((tm, d), lambda i: (i, 0)),
        out_shape=jax.ShapeDtypeStruct((m, d), bf16),
        compiler_params=_params(("parallel",)),
    )(x, norm_w.reshape(1, d))


def _inproj_kernel(hp_ref, hs_ref, w_ref, op_ref, os_ref, wb_scr):
    @pl.when(pl.program_id(1) == 0)
    def _():
        wb_scr[...] = w_ref[...].astype(bf16)
        os_ref[...] = _dot(hs_ref[...], wb_scr[...])

    op_ref[...] = _dot(hp_ref[...], wb_scr[...]).astype(op_ref.dtype)


def _in_proj(h_p, h_s, w, layer, tm, tn):
    m, d = h_p.shape
    ms = h_s.shape[0]
    n = w.shape[2]
    return pl.pallas_call(
        _inproj_kernel,
        grid=(n // tn, m // tm),
        in_specs=[pl.BlockSpec((tm, d), lambda j, i: (i, 0)),
                  pl.BlockSpec((ms, d), lambda j, i: (0, 0)),
                  pl.BlockSpec((None, d, tn), lambda j, i: (layer, 0, j))],
        out_specs=(pl.BlockSpec((tm, tn), lambda j, i: (i, j)),
                   pl.BlockSpec((ms, tn), lambda j, i: (0, j))),
        out_shape=(jax.ShapeDtypeStruct((m, n), PROJ_DTYPE), jax.ShapeDtypeStruct((ms, n), f32)),
        scratch_shapes=[pltpu.VMEM((d, tn), bf16)],
        compiler_params=_params(("parallel", "arbitrary")),
    )(h_p, h_s, w)


def _merge_kernel(ya_ref, yb_ref, yc_ref, wb_ref, ga_ref, gb_ref, gc_ref, o_ref):
    acc = None
    for n, (y_ref, g_ref) in enumerate(((ya_ref, ga_ref), (yb_ref, gb_ref), (yc_ref, gc_ref))):
        t = _sigmoid_tanh(g_ref[...].astype(f32)) * _dot(y_ref[...], wb_ref[n])
        acc = t if acc is None else acc + t
    o_ref[...] = acc.astype(o_ref.dtype)


def _merge(ya, yb, yc, wb_bf, layer, proj, tm, tn):
    m = ya.shape[0]
    gate0 = COL_GATE * BRANCH_W // tn
    per = D_MODEL // tn
    y_spec = pl.BlockSpec((tm, BRANCH_W), lambda i, j: (i, 0))
    g_specs = [pl.BlockSpec((tm, tn), functools.partial(lambda i, j, n: (i, gate0 + n * per + j), n=n))
               for n in range(N_BRANCH)]
    return pl.pallas_call(
        _merge_kernel,
        grid=(m // tm, D_MODEL // tn),
        in_specs=[y_spec, y_spec, y_spec,
                  pl.BlockSpec((None, N_BRANCH, BRANCH_W, tn), lambda i, j: (layer, 0, 0, j))] + g_specs,
        out_specs=pl.BlockSpec((tm, tn), lambda i, j: (i, j)),
        out_shape=jax.ShapeDtypeStruct((m, D_MODEL), bf16),
        compiler_params=_params(("parallel", "parallel")),
    )(ya, yb, yc, wb_bf, proj, proj, proj)


def _outproj_kernel(m_ref, w_ref, x_ref, nw_ref, o_ref, h_ref):
    y = x_ref[...] + _dot(m_ref[...], w_ref[...])
    o_ref[...] = y
    ms = jnp.mean(y * y, axis=-1, keepdims=True)
    h_ref[...] = (y * lax.rsqrt(ms + EPS) * nw_ref[...]).astype(h_ref.dtype)


def _out_proj(mm, w_bf, layer, x, next_norm_w, tm):
    m = x.shape[0]
    row = pl.BlockSpec((tm, D_MODEL), lambda i: (i, 0))
    return pl.pallas_call(
        _outproj_kernel,
        grid=(m // tm,),
        in_specs=[row,
                  pl.BlockSpec((None, D_MODEL, D_MODEL), lambda i: (layer, 0, 0)),
                  row,
                  pl.BlockSpec((1, D_MODEL), lambda i: (0, 0))],
        out_specs=(row, row),
        out_shape=(jax.ShapeDtypeStruct((m, D_MODEL), f32), jax.ShapeDtypeStruct((m, D_MODEL), bf16)),
        compiler_params=_params(("parallel",)),
    )(mm, w_bf, x, next_norm_w.reshape(1, D_MODEL))


def _conv_kernel(ch_ref, cb_ref, cc_ref, cz_ref, w_ref, ya_ref, nc_ref, carry_ref, *, tt):
    @pl.when(pl.program_id(1) == 0)
    def _():
        carry_ref[...] = jnp.zeros_like(carry_ref)

    u = cc_ref[0].astype(f32) * ch_ref[0].astype(f32)
    prev2 = carry_ref[0:1, :]
    prev1 = carry_ref[1:2, :]
    row = lax.broadcasted_iota(jnp.int32, u.shape, 0)
    u1 = jnp.where(row == 0, prev1, pltpu.roll(u, 1, axis=0))
    u2 = jnp.where(row == 0, prev2, jnp.where(row == 1, prev1, pltpu.roll(u, 2, axis=0)))
    y = w_ref[0:1, :] * u2 + w_ref[1:2, :] * u1 + w_ref[2:3, :] * u
    ya_ref[0] = (cb_ref[0].astype(f32) * y * _silu(cz_ref[0].astype(f32))).astype(ya_ref.dtype)
    tail = u[tt - 2:tt, :]
    carry_ref[0:2, :] = tail
    nc_ref[0] = tail


def _conv_prompt(proj, conv_w, tt):
    b, t, _ = proj.shape
    specs = [pl.BlockSpec((1, tt, BRANCH_W), functools.partial(lambda i, j, c: (i, j, c), c=COL_CONV + c))
             for c in range(4)]
    return pl.pallas_call(
        functools.partial(_conv_kernel, tt=tt),
        grid=(b, t // tt),
        in_specs=specs + [pl.BlockSpec((CONV_WIDTH, BRANCH_W), lambda i, j: (0, 0))],
        out_specs=(pl.BlockSpec((1, tt, BRANCH_W), lambda i, j: (i, j, 0)),
                   pl.BlockSpec((1, CONV_WIDTH - 1, BRANCH_W), lambda i, j: (i, 0, 0))),
        out_shape=(jax.ShapeDtypeStruct((b, t, BRANCH_W), bf16),
                   jax.ShapeDtypeStruct((b, CONV_WIDTH - 1, BRANCH_W), f32)),
        scratch_shapes=[pltpu.VMEM((8, BRANCH_W), f32)],
        compiler_params=_params(("parallel", "arbitrary")),
    )(proj, proj, proj, proj, conv_w)


def _rec_gates(rq, rf, lb):
    e = jnp.exp(-jnp.abs(rf))
    r = 1.0 / (1.0 + e)
    er = e * r
    pos = rf >= 0
    sig = jnp.where(pos, r, er)
    nsig = jnp.where(pos, er, r)
    g = jnp.log(lb + (1.0 - lb) * sig)
    k = (1.0 - lb) * nsig
    return _silu(rq), k, g


def _hgrn_kernel(rq_ref, rf_ref, ri_ref, rg_ref, lb_ref, onw_ref, yb_ref, st_ref, st_scr, *, tt, hp, unroll):
    c_len = REC_CHUNK

    @pl.when(pl.program_id(2) == 0)
    def _():
        st_scr[...] = jnp.zeros_like(st_scr)

    onw = onw_ref[...]
    ri = lax.broadcasted_iota(jnp.int32, (c_len, c_len), 0)
    ci = lax.broadcasted_iota(jnp.int32, (c_len, c_len), 1)
    tri = (ci <= ri).astype(bf16)
    ones_kc = jnp.ones((HEAD_W, c_len), bf16)
    rowk = lax.broadcasted_iota(jnp.int32, (c_len, HEAD_W), 0)
    level_masks = []
    s = REC_SUB
    while s < c_len:
        shift = int(math.log2(2 * s))
        level_masks.append((s, ((((ri ^ ci) >> shift) | ((ri & s) ^ s) | (ci & s)) == 0)))
        s *= 2
    sub_shift = int(math.log2(REC_SUB))
    diag_masks = [((((ri - ci) ^ d) | ((ri >> sub_shift) ^ (ci >> sub_shift))) == 0) for d in range(REC_SUB)]
    valid_rows = [(rowk & (REC_SUB - 1)) >= d for d in range(REC_SUB)]

    def head_chunk(r0, hh):
        rows = pl.ds(r0, c_len)
        lanes = slice(hh * HEAD_W, (hh + 1) * HEAD_W)
        q, k, g = _rec_gates(rq_ref[0, rows, lanes].astype(f32), rf_ref[0, rows, lanes].astype(f32),
                             lb_ref[:, lanes])
        v_bf = ri_ref[0, rows, lanes].astype(bf16)
        g_hi = g.astype(bf16)
        rem = g - g_hi.astype(f32)
        g_mid = rem.astype(bf16)
        g_lo = (rem - g_mid.astype(f32)).astype(bf16)
        b = _dot(tri, g_lo) + _dot(tri, g_mid) + _dot(tri, g_hi)
        b_last = b[c_len - 1:c_len, :]

        a = jnp.zeros((c_len, c_len), f32)
        for s, mask in level_masks:
            refs = [jnp.broadcast_to(b[m + s - 1:m + s, :], (2 * s, HEAD_W)) for m in range(0, c_len, 2 * s)]
            z = jnp.exp(-jnp.abs(b - jnp.concatenate(refs, axis=0)))
            a = jnp.where(mask, _dot_nt((q * z).astype(bf16), (k * z).astype(bf16)), a)
        for d in range(REC_SUB):
            if d == 0:
                p = q * k
            else:
                ex = jnp.where(valid_rows[d], b - pltpu.roll(b, d, axis=0), 0.0)
                p = q * pltpu.roll(k, d, axis=0) * jnp.exp(ex)
            a_d = _dot(p.astype(bf16), ones_kc)
            a = jnp.where(diag_masks[d], a_d, a)

        st = st_scr[hh]
        o = _dot(a.astype(bf16), v_bf) + _dot_nt((q * jnp.exp(b)).astype(bf16), st.astype(bf16))
        ms = jnp.mean(o * o, axis=-1, keepdims=True)
        y = o * lax.rsqrt(ms + EPS) * onw * _silu(rg_ref[0, rows, lanes].astype(f32))
        yb_ref[0, rows, lanes] = y.astype(yb_ref.dtype)
        k_dec = (k * jnp.exp(b_last - b)).astype(bf16)
        st_scr[hh] = st * jnp.exp(b_last) + _dot_tn(v_bf, k_dec)

    def chunk(c, carry):
        r0 = pl.multiple_of(c * c_len, c_len)
        for hh in range(hp):
            head_chunk(r0, hh)
        return carry

    lax.fori_loop(0, tt // c_len, chunk, 0, unroll=unroll)

    @pl.when(pl.program_id(2) == pl.num_programs(2) - 1)
    def _():
        st_ref[0] = st_scr[...]


def _hgrn_prompt(proj, lb, out_norm, tt, hp=REC_HEADS_PER_STEP, unroll=REC_UNROLL):
    b, t, _ = proj.shape
    base = COL_REC * N_HEADS // hp
    specs = [pl.BlockSpec((1, tt, hp * HEAD_W),
                          functools.partial(lambda i, h, j, c: (i, j, c + h), c=base + c * N_HEADS // hp))
             for c in range(4)]
    return pl.pallas_call(
        functools.partial(_hgrn_kernel, tt=tt, hp=hp, unroll=unroll),
        grid=(b, N_HEADS // hp, t // tt),
        in_specs=specs + [pl.BlockSpec((1, hp * HEAD_W), lambda i, h, j: (0, h)),
                          pl.BlockSpec((1, HEAD_W), lambda i, h, j: (0, 0))],
        out_specs=(pl.BlockSpec((1, tt, hp * HEAD_W), lambda i, h, j: (i, j, h)),
                   pl.BlockSpec((1, hp, HEAD_W, HEAD_W), lambda i, h, j: (i, h, 0, 0))),
        out_shape=(jax.ShapeDtypeStruct((b, t, BRANCH_W), bf16),
                   jax.ShapeDtypeStruct((b, N_HEADS, HEAD_W, HEAD_W), f32)),
        scratch_shapes=[pltpu.VMEM((hp, HEAD_W, HEAD_W), f32)],
        compiler_params=_params(("parallel", "parallel", "arbitrary")),
    )(proj, proj, proj, proj, lb.reshape(1, BRANCH_W), out_norm.reshape(1, HEAD_W))


def _rope_tables(pos):
    half = ROT_DIM // 2
    inv_freq = ROPE_THETA ** (-jnp.arange(half, dtype=f32) * 2.0 / ROT_DIM)
    ang = pos.astype(f32)[:, None] * inv_freq[None, :]
    cos, sin = jnp.cos(ang), jnp.sin(ang)
    t = pos.shape[0]
    one = jnp.ones((t, ATT_HD - ROT_DIM), f32)
    zero = jnp.zeros((t, ATT_HD - ROT_DIM), f32)
    zh = jnp.zeros((t, half), f32)
    cos_t = jnp.concatenate([cos, cos, one], axis=-1)
    sa = jnp.concatenate([-sin, zh, zero], axis=-1)
    sb = jnp.concatenate([zh, sin, zero], axis=-1)
    return tuple(jnp.concatenate([x, x], axis=-1) for x in (cos_t, sa, sb))


def _component_mean_sq(x, grp):
    sq = x * x
    hi = sq.astype(bf16)
    lo = (sq - hi.astype(f32)).astype(bf16)
    return (_dot(lo, grp) + _dot(hi, grp)) * (1.0 / ATT_HD)


def _qk_norm_rope(x, w, grp, cos_t, sa, sb):
    xn = x * lax.rsqrt(_component_mean_sq(x, grp) + EPS) * w
    half = ROT_DIM // 2
    return xn * cos_t + pltpu.roll(xn, HEAD_W - half, axis=1) * sa + pltpu.roll(xn, half, axis=1) * sb


def _group_matrix():
    r = lax.broadcasted_iota(jnp.int32, (HEAD_W, HEAD_W), 0)
    c = lax.broadcasted_iota(jnp.int32, (HEAD_W, HEAD_W), 1)
    return ((r < ATT_HD) == (c < ATT_HD)).astype(bf16)


def _qkprep_kernel(aq_ref, ak_ref, qn_ref, kn_ref, cos_ref, sa_ref, sb_ref, kf_ref, kb_ref, qz_ref):
    grp = _group_matrix()
    cos_t, sa, sb = cos_ref[...], sa_ref[...], sb_ref[...]
    lane = lax.broadcasted_iota(jnp.int32, cos_t.shape, 1)
    first = lane < ATT_HD
    scale = Q_SCALE
    for h in range(N_HEADS):
        sl = slice(h * HEAD_W, (h + 1) * HEAD_W)
        kk = _qk_norm_rope(ak_ref[0, :, sl].astype(f32), kn_ref[...], grp, cos_t, sa, sb)
        kf_ref[0, :, sl] = kk
        kb_ref[0, :, sl] = kk.astype(bf16)
        qq = _qk_norm_rope(aq_ref[0, :, sl].astype(f32), qn_ref[...], grp, cos_t, sa, sb) * scale
        qz_ref[0, 0, :, sl] = jnp.where(first, qq, 0.0).astype(bf16)
        qz_ref[0, 1, :, sl] = jnp.where(first, 0.0, qq).astype(bf16)


def _qk_prep(proj, q_norm, k_norm, tables, tt):
    b, t, _ = proj.shape
    specs = [pl.BlockSpec((1, tt, BRANCH_W), functools.partial(lambda i, j, c: (i, j, c), c=COL_ATT + c))
             for c in range(2)]
    w_spec = pl.BlockSpec((1, HEAD_W), lambda i, j: (0, 0))
    t_spec = pl.BlockSpec((tt, HEAD_W), lambda i, j: (j, 0))
    o_spec = pl.BlockSpec((1, tt, BRANCH_W), lambda i, j: (i, j, 0))
    tile2 = lambda w: jnp.concatenate([w, w]).reshape(1, HEAD_W)
    return pl.pallas_call(
        _qkprep_kernel,
        grid=(b, t // tt),
        in_specs=specs + [w_spec, w_spec, t_spec, t_spec, t_spec],
        out_specs=(o_spec, o_spec,
                   pl.BlockSpec((1, 2, tt, BRANCH_W), lambda i, j: (i, 0, j, 0))),
        out_shape=(jax.ShapeDtypeStruct((b, t, BRANCH_W), f32),
                   jax.ShapeDtypeStruct((b, t, BRANCH_W), bf16),
                   jax.ShapeDtypeStruct((b, 2, t, BRANCH_W), bf16)),
        compiler_params=_params(("parallel", "parallel")),
    )(proj, proj, tile2(q_norm), tile2(k_norm), *tables)


def _subln_gate(o, subln, coef, az):
    ms = jnp.mean(o * o, axis=-1, keepdims=True)
    return o * lax.rsqrt(ms + EPS) * subln * coef * _silu(az)


def _flash_kernel(qz_ref, k_ref, v_ref, az_ref, lam_ref, sub_ref, o_ref, m_scr, acc_scr,
                  *, tq, coef, hpf):
    rows = min(FLASH_ROWS, tq)
    qi = pl.program_id(2)
    m_scr[...] = jnp.full_like(m_scr, MASK_VALUE)
    acc_scr[...] = jnp.zeros_like(acc_scr)

    def update(ki, on_diagonal):
        keys = pl.ds(pl.multiple_of(ki * tq, tq), tq)
        for hh in range(hpf):
            lanes = slice(hh * HEAD_W, (hh + 1) * HEAD_W)
            k = k_ref[0, keys, lanes]
            v_ones = jnp.concatenate([v_ref[0, keys, lanes], jnp.ones((tq, HEAD_W), bf16)], axis=1)
            for r0 in range(0, 2 * tq, rows):
                comp, q0 = divmod(r0, tq)
                sl = slice(r0, r0 + rows)
                s = _dot_nt(qz_ref[0, comp, q0:q0 + rows, lanes], k)
                if on_diagonal:
                    r = lax.broadcasted_iota(jnp.int32, s.shape, 0)
                    c = lax.broadcasted_iota(jnp.int32, s.shape, 1)
                    s = jnp.where(c <= r + q0, s, MASK_VALUE)
                m_old = m_scr[hh, sl]
                m_new = jnp.maximum(m_old, jnp.max(s, axis=-1, keepdims=True))
                alpha = jnp.exp2(m_old - m_new)
                p = jnp.exp2((s - jnp.concatenate([m_new] * (tq // HEAD_W), axis=1)).astype(bf16))
                acc_scr[hh, sl] = jnp.concatenate([alpha, alpha], axis=1) * acc_scr[hh, sl] + _dot(p, v_ones)
                m_scr[hh, sl] = m_new

    def below_diagonal(ki, carry):
        update(ki, False)
        return carry

    lax.fori_loop(0, qi, below_diagonal, 0)
    update(qi, True)
    for hh in range(hpf):
        lanes = slice(hh * HEAD_W, (hh + 1) * HEAD_W)
        on = acc_scr[hh, :, 0:HEAD_W] / acc_scr[hh, :, HEAD_W:2 * HEAD_W]
        o = on[0:tq] - lam_ref[...] * on[tq:2 * tq]
        o_ref[0, :, lanes] = _subln_gate(o, sub_ref[...], coef,
                                         az_ref[0, :, lanes].astype(f32)).astype(o_ref.dtype)


def _flash_prompt(qz, k_bf, proj, lam_row, subln, coef, tq, hpf=FLASH_HEADS_PER_STEP):
    b, t, _ = k_bf.shape
    n = t // tq
    v_col = (COL_ATT + 2) * N_HEADS // hpf
    az_col = (COL_ATT + 3) * N_HEADS // hpf
    w = hpf * HEAD_W
    return pl.pallas_call(
        functools.partial(_flash_kernel, tq=tq, coef=coef, hpf=hpf),
        grid=(b, N_HEADS // hpf, n),
        in_specs=[pl.BlockSpec((1, 2, tq, w), lambda i, h, q: (i, 0, q, h)),
                  pl.BlockSpec((1, t, w), lambda i, h, q: (i, 0, h)),
                  pl.BlockSpec((1, t, w), lambda i, h, q: (i, 0, v_col + h)),
                  pl.BlockSpec((1, tq, w), lambda i, h, q: (i, q, az_col + h)),
                  pl.BlockSpec((1, HEAD_W), lambda i, h, q: (0, 0)),
                  pl.BlockSpec((1, HEAD_W), lambda i, h, q: (0, 0))],
        out_specs=pl.BlockSpec((1, tq, w), lambda i, h, q: (i, q, h)),
        out_shape=jax.ShapeDtypeStruct((b, t, BRANCH_W), bf16),
        scratch_shapes=[pltpu.VMEM((hpf, 2 * tq, HEAD_W), f32),
                        pltpu.VMEM((hpf, 2 * tq, 2 * HEAD_W), f32)],
        compiler_params=_params(("parallel", "parallel", "parallel")),
    )(qz, k_bf, proj, proj, lam_row, subln.reshape(1, HEAD_W))


def _lane_to_sublane(row):
    r = lax.broadcasted_iota(jnp.int32, (HEAD_W, HEAD_W), 0)
    c = lax.broadcasted_iota(jnp.int32, (HEAD_W, HEAD_W), 1)
    return jnp.sum(jnp.where(r == c, jnp.broadcast_to(row, (HEAD_W, HEAD_W)), 0.0), axis=1, keepdims=True)


def _sample_mix_kernel(p_ref, cbuf_ref, st_ref, cw_ref, lb_ref, onw_ref, qn_ref, kn_ref,
                       cos_ref, sa_ref, sb_ref,
                       ya_ref, yb_ref, nc_ref, nst_ref, kf_ref, qm_ref):
    blk = lambda c: p_ref[0, c * N_HEADS:(c + 1) * N_HEADS, :]
    u = blk(COL_CONV + 2) * blk(COL_CONV)
    y = cw_ref[0] * cbuf_ref[0, 0] + cw_ref[1] * cbuf_ref[0, 1] + cw_ref[2] * u
    ya_ref[0] = (blk(COL_CONV + 1) * y * _silu(blk(COL_CONV + 3))).astype(ya_ref.dtype)
    nc_ref[0, 0] = cbuf_ref[0, 1]
    nc_ref[0, 1] = u
    q, k, g = _rec_gates(blk(COL_REC), blk(COL_REC + 1), lb_ref[...])
    v = blk(COL_REC + 2)
    dec = jnp.exp(g)
    outs = []
    for h in range(N_HEADS):
        st_new = st_ref[0, h] * dec[h:h + 1, :] + _lane_to_sublane(v[h:h + 1, :]) * k[h:h + 1, :]
        nst_ref[0, h] = st_new
        qh = jnp.broadcast_to(q[h:h + 1, :], (8, HEAD_W)).astype(bf16)
        outs.append(_dot_nt(qh, st_new.astype(bf16))[0:1, :])
    o = jnp.concatenate(outs, axis=0)
    ms = jnp.mean(o * o, axis=-1, keepdims=True)
    yb_ref[0] = (o * lax.rsqrt(ms + EPS) * onw_ref[...] * _silu(blk(COL_REC + 3))).astype(yb_ref.dtype)
    grp = _group_matrix()
    cos_t, sa, sb = cos_ref[...], sa_ref[...], sb_ref[...]
    kf_ref[0] = _qk_norm_rope(blk(COL_ATT + 1), kn_ref[...], grp, cos_t, sa, sb)
    qq = _qk_norm_rope(blk(COL_ATT), qn_ref[...], grp, cos_t, sa, sb) * Q_SCALE
    r = lax.broadcasted_iota(jnp.int32, (2 * N_HEADS, HEAD_W), 0)
    lane = lax.broadcasted_iota(jnp.int32, (2 * N_HEADS, HEAD_W), 1)
    for h in range(N_HEADS):
        qh = jnp.broadcast_to(qq[h:h + 1, :], (2 * N_HEADS, HEAD_W))
        keep = r == 2 * h + jnp.where(lane < ATT_HD, 0, 1)
        qm_ref[0, :, h * HEAD_W:(h + 1) * HEAD_W] = jnp.where(keep, qh, 0.0).astype(bf16)


def _sample_mix(proj_s, conv_state, st_t, conv_w, lb, out_norm, q_norm, k_norm, tables):
    nb = proj_s.shape[0]
    rows = N_IN // HEAD_W
    tile2 = lambda w: jnp.concatenate([w, w]).reshape(1, HEAD_W)
    vec = pl.BlockSpec((1, HEAD_W), lambda i: (0, 0))
    hw = (N_HEADS, HEAD_W)
    out_row = lambda dt: (pl.BlockSpec((1,) + hw, lambda i: (i, 0, 0)), jax.ShapeDtypeStruct((nb,) + hw, dt))
    outs = [out_row(f32), out_row(f32),
            (pl.BlockSpec((1, CONV_WIDTH - 1) + hw, lambda i: (i, 0, 0, 0)),
             jax.ShapeDtypeStruct((nb, CONV_WIDTH - 1) + hw, f32)),
            (pl.BlockSpec((1, N_HEADS, HEAD_W, HEAD_W), lambda i: (i, 0, 0, 0)),
             jax.ShapeDtypeStruct((nb, N_HEADS, HEAD_W, HEAD_W), f32)),
            out_row(f32),
            (pl.BlockSpec((1, 2 * N_HEADS, BRANCH_W), lambda i: (i, 0, 0)),
             jax.ShapeDtypeStruct((nb, 2 * N_HEADS, BRANCH_W), bf16))]
    return pl.pallas_call(
        _sample_mix_kernel,
        grid=(nb,),
        in_specs=[pl.BlockSpec((1, rows, HEAD_W), lambda i: (i, 0, 0)),
                  pl.BlockSpec((1, CONV_WIDTH - 1) + hw, lambda i: (i, 0, 0, 0)),
                  pl.BlockSpec((1, N_HEADS, HEAD_W, HEAD_W), lambda i: (i, 0, 0, 0)),
                  pl.BlockSpec((CONV_WIDTH,) + hw, lambda i: (0, 0, 0)),
                  pl.BlockSpec(hw, lambda i: (0, 0)),
                  vec, vec, vec, vec, vec, vec],
        out_specs=tuple(o[0] for o in outs),
        out_shape=tuple(o[1] for o in outs),
        compiler_params=_params(("parallel",)),
    )(proj_s.reshape(nb, rows, HEAD_W), conv_state.reshape((nb, CONV_WIDTH - 1) + hw), st_t,
      conv_w.reshape((CONV_WIDTH,) + hw), lb.reshape(hw), out_norm.reshape(1, HEAD_W),
      tile2(q_norm), tile2(k_norm), *tables)


def _decode_kernel(pt_ref, qm_ref, kn_ref, vn_ref, az_ref, lam_ref, sub_ref, *rest, coef):
    g = PAGES_PER_STEP
    k_refs, v_refs = rest[:g], rest[g:2 * g]
    o_ref, m_scr, l_scr, acc_scr, fin_scr, exp_scr = rest[2 * g:]
    step = pl.program_id(1)
    q = qm_ref[0]
    n_rows = PAGE_SIZE * N_HEADS
    head_shift = int(math.log2(N_HEADS))

    @pl.when(step == 0)
    def _():
        s0 = jnp.sum(q.astype(f32) * kn_ref[0], axis=-1, keepdims=True)
        m_scr[...] = jnp.broadcast_to(s0, m_scr.shape)
        l_scr[...] = jnp.ones_like(l_scr)
        acc_scr[...] = vn_ref[0]
        pos = lax.broadcasted_iota(jnp.int32, (PAGE_SIZE, n_rows), 0)
        col = lax.broadcasted_iota(jnp.int32, (PAGE_SIZE, n_rows), 1)
        exp_scr[...] = jnp.where((col >> head_shift) == pos, 1.0, 0.0).astype(bf16)

    s = jnp.concatenate([_dot(q, k_refs[i][0, 0].astype(bf16)) for i in range(g)], axis=-1)
    m_old = m_scr[...]
    m_new = jnp.maximum(m_old, jnp.max(s, axis=-1, keepdims=True))
    alpha = jnp.exp2(m_old - m_new)
    p = jnp.exp2(s - jnp.concatenate([m_new] * g, axis=1))
    l_scr[...] = alpha * l_scr[...] + jnp.sum(p, axis=-1, keepdims=True)
    n_q = 2 * N_HEADS
    p_bf = p.astype(bf16)
    stacked = jnp.concatenate([p_bf[:, i * PAGE_SIZE:(i + 1) * PAGE_SIZE] for i in range(g)], axis=0)
    r = lax.broadcasted_iota(jnp.int32, (g * n_q, n_rows), 0)
    col = lax.broadcasted_iota(jnp.int32, (g * n_q, n_rows), 1)
    own_head = (col & (N_HEADS - 1)) == ((r & (n_q - 1)) >> 1)
    spread = jnp.where(own_head, _dot(stacked, exp_scr[...]), 0.0).astype(bf16)
    pv = None
    for i in range(g):
        t = _dot(spread[i * n_q:(i + 1) * n_q], v_refs[i][0, 0].astype(bf16))
        pv = t if pv is None else pv + t
    acc_scr[...] = alpha * acc_scr[...] + pv
    m_scr[...] = m_new

    @pl.when(step == pl.num_programs(1) - 1)
    def _():
        row = lax.broadcasted_iota(jnp.int32, acc_scr.shape, 0)
        fin_scr[...] = acc_scr[...] / l_scr[...] * jnp.where((row & 1) == 0, 1.0, -lam_ref[...])
        o = fin_scr[pl.ds(0, N_HEADS, stride=2), :] + fin_scr[pl.ds(1, N_HEADS, stride=2), :]
        o_ref[0] = _subln_gate(o, sub_ref[...], coef, az_ref[0]).astype(o_ref.dtype)


def _decode_attention(page_table, qmat, k_new, v_new, az, lam_row, subln, cache_k, cache_v, layer, coef):
    nb, n_pages = page_table.shape
    g = PAGES_PER_STEP
    per_batch = lambda rows, width: pl.BlockSpec((1, rows, width), lambda b, s, pt: (b, 0, 0))
    vec = pl.BlockSpec((1, HEAD_W), lambda b, s, pt: (0, 0))
    page = lambda rows, width: [
        pl.BlockSpec((1, 1, rows, width),
                     functools.partial(lambda b, s, pt, i: (layer, pt[b, s * g + i], 0, 0), i=i))
        for i in range(g)]
    acc = pltpu.VMEM((2 * N_HEADS, HEAD_W), f32)
    grid_spec = pltpu.PrefetchScalarGridSpec(
        num_scalar_prefetch=1,
        grid=(nb, n_pages // g),
        in_specs=[per_batch(2 * N_HEADS, BRANCH_W), per_batch(1, BRANCH_W), per_batch(2 * N_HEADS, HEAD_W),
                  per_batch(N_HEADS, HEAD_W), vec, vec]
        + page(BRANCH_W, PAGE_SIZE) + page(PAGE_SIZE * N_HEADS, HEAD_W),
        out_specs=per_batch(N_HEADS, HEAD_W),
        scratch_shapes=[acc, acc, acc, acc, pltpu.VMEM((PAGE_SIZE, PAGE_SIZE * N_HEADS), bf16)])
    return pl.pallas_call(
        functools.partial(_decode_kernel, coef=coef),
        grid_spec=grid_spec,
        out_shape=jax.ShapeDtypeStruct((nb, N_HEADS, HEAD_W), f32),
        compiler_params=_params(("parallel", "arbitrary")),
    )(page_table, qmat, k_new, v_new, az, lam_row, subln.reshape(1, HEAD_W),
      *([cache_k] * g), *([cache_v] * g))


def kernel(x_prompt, x_sample, cache_k, cache_v, state_conv, state_rec, page_table, norm_w, w_in, conv_w,
           rec_lb_logits, rec_out_norm, q_norm, k_norm, lambda_qk, attn_subln, w_branch, w_out):
    nbp, seq, _ = x_prompt.shape
    nbs = x_sample.shape[0]
    n_pool = cache_k.shape[1]
    lb_all, lam_all = _layer_params(rec_lb_logits, lambda_qk)
    w_branch_bf, w_out_bf = w_branch.astype(bf16), w_out.astype(bf16)
    ck = jnp.transpose(cache_k, (0, 1, 3, 4, 5, 2)).reshape(DEPTH, n_pool, BRANCH_W, PAGE_SIZE)
    cv = cache_v.reshape(DEPTH, n_pool, PAGE_SIZE * N_HEADS, HEAD_W)
    tab_p = _rope_tables(jnp.arange(seq, dtype=jnp.int32))
    tab_s = _rope_tables(jnp.full((1,), page_table.shape[1] * PAGE_SIZE, jnp.int32))

    mp = nbp * seq
    tm = min(1024, mp)
    tt = min(512, seq)
    y_p = x_prompt.reshape(mp, D_MODEL)
    ms = -(-nbs // BF16_SUBLANES) * BF16_SUBLANES
    pad_rows = lambda a: jnp.pad(a, ((0, ms - nbs), (0, 0)))
    y_s = pad_rows(x_sample.reshape(nbs, D_MODEL))
    h_p, h_s = _rmsnorm_bf16(y_p, norm_w[0], tm), _rmsnorm_bf16(y_s, norm_w[0], ms)
    outs = [[] for _ in range(8)]
    for l in range(DEPTH):
        coef = 1.0 - (0.8 - 0.6 * math.exp(-0.3 * l))
        lam_row = lam_all[l:l + 1]

        next_norm = norm_w[min(l + 1, DEPTH - 1)]
        proj, proj_pad = _in_proj(h_p, h_s, w_in, l, tm, 1024)
        proj3 = proj.reshape(nbp, seq, N_IN)
        ya, c_p = _conv_prompt(proj3, conv_w[l], tt)
        yb, r_p = _hgrn_prompt(proj3, lb_all[l], rec_out_norm[l], tt)
        k_p, k_bf, qz = _qk_prep(proj3, q_norm[l], k_norm[l], tab_p, min(256, seq))
        yc = _flash_prompt(qz, k_bf, proj3, lam_row, attn_subln[l], coef, tt)
        mm = _merge(ya.reshape(mp, BRANCH_W), yb.reshape(mp, BRANCH_W), yc.reshape(mp, BRANCH_W),
                    w_branch_bf, l, proj, min(256, mp), D_MODEL)
        y_p, h_p = _out_proj(mm, w_out_bf, l, y_p, next_norm, min(256, mp))
        v_p = proj3[:, :, (COL_ATT + 2) * BRANCH_W:(COL_ATT + 3) * BRANCH_W].astype(f32)

        proj_s = proj_pad[:nbs]
        ya_s, yb_s, c_s, st_s, k_s, qmat = _sample_mix(
            proj_s, state_conv[l], jnp.swapaxes(state_rec[l], -1, -2), conv_w[l], lb_all[l],
            rec_out_norm[l], q_norm[l], k_norm[l], tab_s)
        v_s = proj_s[:, (COL_ATT + 2) * BRANCH_W:(COL_ATT + 3) * BRANCH_W]
        az_s = proj_s[:, (COL_ATT + 3) * BRANCH_W:(COL_ATT + 4) * BRANCH_W]
        v_rows = jnp.repeat(v_s.reshape(nbs, N_HEADS, HEAD_W), 2, axis=1)
        yc_s = _decode_attention(page_table, qmat, k_s.reshape(nbs, 1, BRANCH_W), v_rows,
                                 az_s.reshape(nbs, N_HEADS, HEAD_W), lam_row, attn_subln[l], ck, cv, l, coef)
        mm_s = _merge(pad_rows(ya_s.reshape(nbs, BRANCH_W)).astype(bf16),
                      pad_rows(yb_s.reshape(nbs, BRANCH_W)).astype(bf16),
                      pad_rows(yc_s.reshape(nbs, BRANCH_W)).astype(bf16),
                      w_branch_bf, l, proj_pad, ms, D_MODEL)
        y_s, h_s = _out_proj(mm_s, w_out_bf, l, y_s, next_norm, ms)

        for lst, val in zip(outs, (
                k_p.reshape(nbp, seq, N_HEADS, 2, ATT_HD), v_p.reshape(nbp, seq, N_HEADS, HEAD_W),
                k_s.reshape(nbs, 1, N_HEADS, 2, ATT_HD), v_s.reshape(nbs, 1, N_HEADS, HEAD_W),
                c_p, c_s.reshape(nbs, CONV_WIDTH - 1, BRANCH_W),
                jnp.swapaxes(r_p, -1, -2), jnp.swapaxes(st_s, -1, -2))):
            lst.append(val)

    return (y_p.reshape(nbp, seq, D_MODEL), y_s[:nbs].reshape(nbs, 1, D_MODEL)) + tuple(jnp.stack(o, axis=0) for o in outs)
```

```python
import functools
import math

import jax
import jax.numpy as jnp
import numpy as np
from jax import lax
from jax.experimental import pallas as pl
from jax.experimental.pallas import tpu as pltpu

D_MODEL = 2048
DEPTH = 4
PAST_LEN = 16384
PAGE_SIZE = 128
BRANCH_W = D_MODEL // 2
CONV_WIDTH = 3
N_HEADS = 8
HEAD_W = BRANCH_W // N_HEADS
ATT_HD = HEAD_W // 2
ROT_DIM = ATT_HD // 4
ROPE_THETA = 500000.0
N_BRANCH = 3
EPS = 1e-6
MASK_VALUE = -1e30
N_IN = 12 * BRANCH_W + N_BRANCH * D_MODEL
COL_CONV, COL_REC, COL_ATT, COL_GATE = 0, 4, 8, 12

V7X_VMEM_BYTES = 64 * 1024 * 1024
BF16_SUBLANES = 16
VMEM_LIMIT = 56 * 1024 * 1024
REC_CHUNK = 128
REC_SUB = 2
REC_HEADS_PER_STEP = 8
REC_UNROLL = 1
LOG2E = math.log2(math.e)
Q_SCALE = ATT_HD ** -0.5 * LOG2E
FLASH_ROWS = 256
FLASH_HEADS_PER_STEP = 8
PAGES_PER_STEP = 8
W_SLABS = 4

f32 = jnp.float32
bf16 = jnp.bfloat16
PROJ_DTYPE = bf16


def _params(sem, vmem=VMEM_LIMIT):
    return pltpu.CompilerParams(dimension_semantics=sem, vmem_limit_bytes=vmem)


def _sigmoid(x):
    return 1.0 / (1.0 + jnp.exp(-x))


def _sigmoid_tanh(x):
    return 0.5 * jnp.tanh(0.5 * x) + 0.5


def _silu(x):
    return x * _sigmoid(x)


def _dot(a, b):
    return jnp.dot(a, b, preferred_element_type=f32)


def _dot_nt(a, b):
    return lax.dot_general(a, b, (((1,), (1,)), ((), ())), preferred_element_type=f32)


def _dot_tn(a, b):
    return lax.dot_general(a, b, (((0,), (0,)), ((), ())), preferred_element_type=f32)


def _param_kernel(logit_ref, lq_ref, lb_ref, lam_ref):
    x = logit_ref[...]
    m = jnp.max(x, axis=0, keepdims=True)
    e = jnp.exp(x - m)
    soft = e / jnp.sum(e, axis=0, keepdims=True)
    run = soft[0:1]
    rows = [run - soft[0:1]]
    for l in range(1, DEPTH):
        run = run + soft[l:l + 1]
        rows.append(run - soft[0:1])
    lb_ref[...] = jnp.concatenate(rows, axis=0)
    lams = []
    for l in range(DEPTH):
        lq = lq_ref[l]
        a = jnp.sum(lq[0:1] * lq[1:2], axis=-1, keepdims=True)
        b = jnp.sum(lq[2:3] * lq[3:4], axis=-1, keepdims=True)
        lam_init = 0.8 - 0.6 * math.exp(-0.3 * l)
        lams.append(jnp.broadcast_to(jnp.exp(a) - jnp.exp(b) + lam_init, (1, HEAD_W)))
    lam_ref[...] = jnp.concatenate(lams, axis=0)


def _layer_params(rec_lb_logits, lambda_qk):
    return pl.pallas_call(
        _param_kernel,
        out_shape=(jax.ShapeDtypeStruct(rec_lb_logits.shape, f32),
                   jax.ShapeDtypeStruct((DEPTH, HEAD_W), f32)),
    )(rec_lb_logits.astype(f32), lambda_qk.astype(f32))


def _rmsnorm_kernel(x_ref, nw_ref, h_ref):
    x = x_ref[...]
    ms = jnp.mean(x * x, axis=-1, keepdims=True)
    h_ref[...] = (x * lax.rsqrt(ms + EPS) * nw_ref[...]).astype(h_ref.dtype)


def _rmsnorm_bf16(x, norm_w, tm):
    m, d = x.shape
    return pl.pallas_call(
        _rmsnorm_kernel,
        grid=(m // tm,),
        in_specs=[pl.BlockSpec((tm, d), lambda i: (i, 0)), pl.BlockSpec((1, d), lambda i: (0, 0))],
        out_specs=pl.BlockSpec((tm, d), lambda i: (i, 0)),
        out_shape=jax.ShapeDtypeStruct((m, d), bf16),
        compiler_params=_params(("parallel",)),
    )(x, norm_w.reshape(1, d))


def _inproj_kernel(hp_ref, hs_ref, *rest):
    w_refs, (op_ref, os_ref, wb_scr) = rest[:W_SLABS], rest[W_SLABS:]
    slab = wb_scr.shape[0] // W_SLABS

    @pl.when(pl.program_id(1) == 0)
    def _():
        for q, w_ref in enumerate(w_refs):
            wb_scr[q * slab:(q + 1) * slab, :] = w_ref[...].astype(bf16)
        os_ref[...] = _dot(hs_ref[...], wb_scr[...])

    op_ref[...] = _dot(hp_ref[...], wb_scr[...]).astype(op_ref.dtype)


def _in_proj(h_p, h_s, w, layer, tm, tn):
    m, d = h_p.shape
    ms = h_s.shape[0]
    n = w.shape[2]
    n_j = n // tn
    w_specs = [pl.BlockSpec((None, d // W_SLABS, tn),
                            functools.partial(lambda j, i, q: (layer, q, jnp.minimum(j + (i > q), n_j - 1)), q=q))
               for q in range(W_SLABS)]
    return pl.pallas_call(
        _inproj_kernel,
        grid=(n_j, m // tm),
        in_specs=[pl.BlockSpec((tm, d), lambda j, i: (i, 0)),
                  pl.BlockSpec((ms, d), lambda j, i: (0, 0))] + w_specs,
        out_specs=(pl.BlockSpec((tm, tn), lambda j, i: (i, j)),
                   pl.BlockSpec((ms, tn), lambda j, i: (0, j))),
        out_shape=(jax.ShapeDtypeStruct((m, n), PROJ_DTYPE), jax.ShapeDtypeStruct((ms, n), f32)),
        scratch_shapes=[pltpu.VMEM((d, tn), bf16)],
        compiler_params=_params(("parallel", "arbitrary")),
    )(h_p, h_s, *([w] * W_SLABS))


def _merge_kernel(ya_ref, yb_ref, yc_ref, wb_ref, ga_ref, gb_ref, gc_ref, o_ref):
    acc = None
    for n, (y_ref, g_ref) in enumerate(((ya_ref, ga_ref), (yb_ref, gb_ref), (yc_ref, gc_ref))):
        t = _sigmoid_tanh(g_ref[...].astype(f32)) * _dot(y_ref[...], wb_ref[n])
        acc = t if acc is None else acc + t
    o_ref[...] = acc.astype(o_ref.dtype)


def _merge(ya, yb, yc, wb_bf, layer, proj, tm, tn):
    m = ya.shape[0]
    gate0 = COL_GATE * BRANCH_W // tn
    per = D_MODEL // tn
    y_spec = pl.BlockSpec((tm, BRANCH_W), lambda i, j: (i, 0))
    g_specs = [pl.BlockSpec((tm, tn), functools.partial(lambda i, j, n: (i, gate0 + n * per + j), n=n))
               for n in range(N_BRANCH)]
    return pl.pallas_call(
        _merge_kernel,
        grid=(m // tm, D_MODEL // tn),
        in_specs=[y_spec, y_spec, y_spec,
                  pl.BlockSpec((None, N_BRANCH, BRANCH_W, tn), lambda i, j: (layer, 0, 0, j))] + g_specs,
        out_specs=pl.BlockSpec((tm, tn), lambda i, j: (i, j)),
        out_shape=jax.ShapeDtypeStruct((m, D_MODEL), bf16),
        compiler_params=_params(("parallel", "parallel")),
    )(ya, yb, yc, wb_bf, proj, proj, proj)


def _outproj_kernel(m_ref, w_ref, x_ref, nw_ref, o_ref, h_ref):
    y = x_ref[...] + _dot(m_ref[...], w_ref[...])
    o_ref[...] = y
    ms = jnp.mean(y * y, axis=-1, keepdims=True)
    h_ref[...] = (y * lax.rsqrt(ms + EPS) * nw_ref[...]).astype(h_ref.dtype)


def _out_proj(mm, w_bf, layer, x, next_norm_w, tm):
    m = x.shape[0]
    row = pl.BlockSpec((tm, D_MODEL), lambda i: (i, 0))
    return pl.pallas_call(
        _outproj_kernel,
        grid=(m // tm,),
        in_specs=[row,
                  pl.BlockSpec((None, D_MODEL, D_MODEL), lambda i: (layer, 0, 0)),
                  row,
                  pl.BlockSpec((1, D_MODEL), lambda i: (0, 0))],
        out_specs=(row, row),
        out_shape=(jax.ShapeDtypeStruct((m, D_MODEL), f32), jax.ShapeDtypeStruct((m, D_MODEL), bf16)),
        compiler_params=_params(("parallel",)),
    )(mm, w_bf, x, next_norm_w.reshape(1, D_MODEL))


def _conv_kernel(ch_ref, cb_ref, cc_ref, cz_ref, w_ref, ya_ref, nc_ref, carry_ref, *, tt):
    @pl.when(pl.program_id(1) == 0)
    def _():
        carry_ref[...] = jnp.zeros_like(carry_ref)

    u = cc_ref[0].astype(f32) * ch_ref[0].astype(f32)
    prev2 = carry_ref[0:1, :]
    prev1 = carry_ref[1:2, :]
    row = lax.broadcasted_iota(jnp.int32, u.shape, 0)
    u1 = jnp.where(row == 0, prev1, pltpu.roll(u, 1, axis=0))
    u2 = jnp.where(row == 0, prev2, jnp.where(row == 1, prev1, pltpu.roll(u, 2, axis=0)))
    y = w_ref[0:1, :] * u2 + w_ref[1:2, :] * u1 + w_ref[2:3, :] * u
    ya_ref[0] = (cb_ref[0].astype(f32) * y * _silu(cz_ref[0].astype(f32))).astype(ya_ref.dtype)
    tail = u[tt - 2:tt, :]
    carry_ref[0:2, :] = tail
    nc_ref[0] = tail


def _conv_prompt(proj, conv_w, tt):
    b, t, _ = proj.shape
    specs = [pl.BlockSpec((1, tt, BRANCH_W), functools.partial(lambda i, j, c: (i, j, c), c=COL_CONV + c))
             for c in range(4)]
    return pl.pallas_call(
        functools.partial(_conv_kernel, tt=tt),
        grid=(b, t // tt),
        in_specs=specs + [pl.BlockSpec((CONV_WIDTH, BRANCH_W), lambda i, j: (0, 0))],
        out_specs=(pl.BlockSpec((1, tt, BRANCH_W), lambda i, j: (i, j, 0)),
                   pl.BlockSpec((1, CONV_WIDTH - 1, BRANCH_W), lambda i, j: (i, 0, 0))),
        out_shape=(jax.ShapeDtypeStruct((b, t, BRANCH_W), bf16),
                   jax.ShapeDtypeStruct((b, CONV_WIDTH - 1, BRANCH_W), f32)),
        scratch_shapes=[pltpu.VMEM((8, BRANCH_W), f32)],
        compiler_params=_params(("parallel", "arbitrary")),
    )(proj, proj, proj, proj, conv_w)


def _rec_gates(rq, rf, lb):
    e = jnp.exp(-jnp.abs(rf))
    r = 1.0 / (1.0 + e)
    er = e * r
    pos = rf >= 0
    sig = jnp.where(pos, r, er)
    nsig = jnp.where(pos, er, r)
    g = jnp.log(lb + (1.0 - lb) * sig) * LOG2E
    k = (1.0 - lb) * nsig
    return _silu(rq), k, g


def _hgrn_kernel(rq_ref, rf_ref, ri_ref, rg_ref, lb_ref, onw_ref, yb_ref, st_ref, st_scr, *, tt, hp, unroll):
    c_len = REC_CHUNK

    @pl.when(pl.program_id(2) == 0)
    def _():
        st_scr[...] = jnp.zeros_like(st_scr)

    onw = onw_ref[...]
    ri = lax.broadcasted_iota(jnp.int32, (c_len, c_len), 0)
    ci = lax.broadcasted_iota(jnp.int32, (c_len, c_len), 1)
    tri = (ci <= ri).astype(bf16)
    ones_kc = jnp.ones((HEAD_W, c_len), bf16)
    rowk = lax.broadcasted_iota(jnp.int32, (c_len, HEAD_W), 0)
    level_masks = []
    s = REC_SUB
    while s < c_len:
        shift = int(math.log2(2 * s))
        level_masks.append((s, ((((ri ^ ci) >> shift) | ((ri & s) ^ s) | (ci & s)) == 0)))
        s *= 2
    sub_shift = int(math.log2(REC_SUB))
    diag_masks = [((((ri - ci) ^ d) | ((ri >> sub_shift) ^ (ci >> sub_shift))) == 0) for d in range(REC_SUB)]
    valid_rows = [(rowk & (REC_SUB - 1)) >= d for d in range(REC_SUB)]

    def head_chunk(r0, hh):
        rows = pl.ds(r0, c_len)
        lanes = slice(hh * HEAD_W, (hh + 1) * HEAD_W)
        q, k, g = _rec_gates(rq_ref[0, rows, lanes].astype(f32), rf_ref[0, rows, lanes].astype(f32),
                             lb_ref[:, lanes])
        v_bf = ri_ref[0, rows, lanes].astype(bf16)
        g_hi = g.astype(bf16)
        rem = g - g_hi.astype(f32)
        g_mid = rem.astype(bf16)
        g_lo = (rem - g_mid.astype(f32)).astype(bf16)
        b = _dot(tri, g_lo) + _dot(tri, g_mid) + _dot(tri, g_hi)
        b_last = b[c_len - 1:c_len, :]

        a = jnp.zeros((c_len, c_len), f32)
        for s, mask in level_masks:
            refs = [jnp.broadcast_to(b[m + s - 1:m + s, :], (2 * s, HEAD_W)) for m in range(0, c_len, 2 * s)]
            z = jnp.exp2(-jnp.abs(b - jnp.concatenate(refs, axis=0)))
            a = jnp.where(mask, _dot_nt((q * z).astype(bf16), (k * z).astype(bf16)), a)
        for d in range(REC_SUB):
            if d == 0:
                p = q * k
            else:
                ex = jnp.where(valid_rows[d], b - pltpu.roll(b, d, axis=0), 0.0)
                p = q * pltpu.roll(k, d, axis=0) * jnp.exp2(ex)
            a_d = _dot(p.astype(bf16), ones_kc)
            a = jnp.where(diag_masks[d], a_d, a)

        st = st_scr[hh]
        o = _dot(a.astype(bf16), v_bf) + _dot_nt((q * jnp.exp2(b)).astype(bf16), st.astype(bf16))
        ms = jnp.mean(o * o, axis=-1, keepdims=True)
        y = o * lax.rsqrt(ms + EPS) * onw * _silu(rg_ref[0, rows, lanes].astype(f32))
        yb_ref[0, rows, lanes] = y.astype(yb_ref.dtype)
        k_dec = (k * jnp.exp2(b_last - b)).astype(bf16)
        st_scr[hh] = st * jnp.exp2(b_last) + _dot_tn(v_bf, k_dec)

    def chunk(c, carry):
        r0 = pl.multiple_of(c * c_len, c_len)
        for hh in range(hp):
            head_chunk(r0, hh)
        return carry

    lax.fori_loop(0, tt // c_len, chunk, 0, unroll=unroll)

    @pl.when(pl.program_id(2) == pl.num_programs(2) - 1)
    def _():
        st_ref[0] = st_scr[...]


def _hgrn_prompt(proj, lb, out_norm, tt, hp=REC_HEADS_PER_STEP, unroll=REC_UNROLL):
    b, t, _ = proj.shape
    base = COL_REC * N_HEADS // hp
    specs = [pl.BlockSpec((1, tt, hp * HEAD_W),
                          functools.partial(lambda i, h, j, c: (i, j, c + h), c=base + c * N_HEADS // hp))
             for c in range(4)]
    return pl.pallas_call(
        functools.partial(_hgrn_kernel, tt=tt, hp=hp, unroll=unroll),
        grid=(b, N_HEADS // hp, t // tt),
        in_specs=specs + [pl.BlockSpec((1, hp * HEAD_W), lambda i, h, j: (0, h)),
                          pl.BlockSpec((1, HEAD_W), lambda i, h, j: (0, 0))],
        out_specs=(pl.BlockSpec((1, tt, hp * HEAD_W), lambda i, h, j: (i, j, h)),
                   pl.BlockSpec((1, hp, HEAD_W, HEAD_W), lambda i, h, j: (i, h, 0, 0))),
        out_shape=(jax.ShapeDtypeStruct((b, t, BRANCH_W), bf16),
                   jax.ShapeDtypeStruct((b, N_HEADS, HEAD_W, HEAD_W), f32)),
        scratch_shapes=[pltpu.VMEM((hp, HEAD_W, HEAD_W), f32)],
        compiler_params=_params(("parallel", "parallel", "arbitrary")),
    )(proj, proj, proj, proj, lb.reshape(1, BRANCH_W), out_norm.reshape(1, HEAD_W))


def _rope_tables(pos):
    half = ROT_DIM // 2
    inv_freq = ROPE_THETA ** (-jnp.arange(half, dtype=f32) * 2.0 / ROT_DIM)
    ang = pos.astype(f32)[:, None] * inv_freq[None, :]
    cos, sin = jnp.cos(ang), jnp.sin(ang)
    t = pos.shape[0]
    one = jnp.ones((t, ATT_HD - ROT_DIM), f32)
    zero = jnp.zeros((t, ATT_HD - ROT_DIM), f32)
    zh = jnp.zeros((t, half), f32)
    cos_t = jnp.concatenate([cos, cos, one], axis=-1)
    sa = jnp.concatenate([-sin, zh, zero], axis=-1)
    sb = jnp.concatenate([zh, sin, zero], axis=-1)
    return tuple(jnp.concatenate([x, x], axis=-1) for x in (cos_t, sa, sb))


def _component_mean_sq(x, grp):
    sq = x * x
    hi = sq.astype(bf16)
    lo = (sq - hi.astype(f32)).astype(bf16)
    return (_dot(lo, grp) + _dot(hi, grp)) * (1.0 / ATT_HD)


def _qk_norm_rope(x, w, grp, cos_t, sa, sb):
    xn = x * lax.rsqrt(_component_mean_sq(x, grp) + EPS) * w
    half = ROT_DIM // 2
    return xn * cos_t + pltpu.roll(xn, HEAD_W - half, axis=1) * sa + pltpu.roll(xn, half, axis=1) * sb


def _group_matrix():
    r = lax.broadcasted_iota(jnp.int32, (HEAD_W, HEAD_W), 0)
    c = lax.broadcasted_iota(jnp.int32, (HEAD_W, HEAD_W), 1)
    return ((r < ATT_HD) == (c < ATT_HD)).astype(bf16)


def _qkprep_kernel(aq_ref, ak_ref, qn_ref, kn_ref, cos_ref, sa_ref, sb_ref, kf_ref, kb_ref, qz_ref):
    grp = _group_matrix()
    cos_t, sa, sb = cos_ref[...], sa_ref[...], sb_ref[...]
    lane = lax.broadcasted_iota(jnp.int32, cos_t.shape, 1)
    first = lane < ATT_HD
    scale = Q_SCALE
    for h in range(N_HEADS):
        sl = slice(h * HEAD_W, (h + 1) * HEAD_W)
        kk = _qk_norm_rope(ak_ref[0, :, sl].astype(f32), kn_ref[...], grp, cos_t, sa, sb)
        kf_ref[0, :, sl] = kk
        kb_ref[0, :, sl] = kk.astype(bf16)
        qq = _qk_norm_rope(aq_ref[0, :, sl].astype(f32), qn_ref[...], grp, cos_t, sa, sb) * scale
        qz_ref[0, 0, :, sl] = jnp.where(first, qq, 0.0).astype(bf16)
        qz_ref[0, 1, :, sl] = jnp.where(first, 0.0, qq).astype(bf16)


def _qk_prep(proj, q_norm, k_norm, tables, tt):
    b, t, _ = proj.shape
    specs = [pl.BlockSpec((1, tt, BRANCH_W), functools.partial(lambda i, j, c: (i, j, c), c=COL_ATT + c))
             for c in range(2)]
    w_spec = pl.BlockSpec((1, HEAD_W), lambda i, j: (0, 0))
    t_spec = pl.BlockSpec((tt, HEAD_W), lambda i, j: (j, 0))
    o_spec = pl.BlockSpec((1, tt, BRANCH_W), lambda i, j: (i, j, 0))
    tile2 = lambda w: jnp.concatenate([w, w]).reshape(1, HEAD_W)
    return pl.pallas_call(
        _qkprep_kernel,
        grid=(b, t // tt),
        in_specs=specs + [w_spec, w_spec, t_spec, t_spec, t_spec],
        out_specs=(o_spec, o_spec,
                   pl.BlockSpec((1, 2, tt, BRANCH_W), lambda i, j: (i, 0, j, 0))),
        out_shape=(jax.ShapeDtypeStruct((b, t, BRANCH_W), f32),
                   jax.ShapeDtypeStruct((b, t, BRANCH_W), bf16),
                   jax.ShapeDtypeStruct((b, 2, t, BRANCH_W), bf16)),
        compiler_params=_params(("parallel", "parallel")),
    )(proj, proj, tile2(q_norm), tile2(k_norm), *tables)


def _subln_gate(o, subln, coef, az):
    ms = jnp.mean(o * o, axis=-1, keepdims=True)
    return o * lax.rsqrt(ms + EPS) * subln * coef * _silu(az)


def _flash_kernel(qz_ref, k_ref, v_ref, az_ref, lam_ref, sub_ref, o_ref, m_scr, acc_scr,
                  *, tq, coef, hpf):
    rows = min(FLASH_ROWS, tq)
    qi = pl.program_id(2)
    m_scr[...] = jnp.full_like(m_scr, MASK_VALUE)
    acc_scr[...] = jnp.zeros_like(acc_scr)

    def update(ki, on_diagonal):
        keys = pl.ds(pl.multiple_of(ki * tq, tq), tq)
        for hh in range(hpf):
            lanes = slice(hh * HEAD_W, (hh + 1) * HEAD_W)
            k = k_ref[0, keys, lanes]
            v_ones = jnp.concatenate([v_ref[0, keys, lanes], jnp.ones((tq, HEAD_W), bf16)], axis=1)
            for r0 in range(0, 2 * tq, rows):
                comp, q0 = divmod(r0, tq)
                sl = slice(r0, r0 + rows)
                s = _dot_nt(qz_ref[0, comp, q0:q0 + rows, lanes], k)
                if on_diagonal:
                    r = lax.broadcasted_iota(jnp.int32, s.shape, 0)
                    c = lax.broadcasted_iota(jnp.int32, s.shape, 1)
                    s = jnp.where(c <= r + q0, s, MASK_VALUE)
                m_old = m_scr[hh, sl]
                m_new = jnp.maximum(m_old, jnp.max(s, axis=-1, keepdims=True))
                alpha = jnp.exp2(m_old - m_new)
                p = jnp.exp2((s - jnp.concatenate([m_new] * (tq // HEAD_W), axis=1)).astype(bf16))
                acc_scr[hh, sl] = jnp.concatenate([alpha, alpha], axis=1) * acc_scr[hh, sl] + _dot(p, v_ones)
                m_scr[hh, sl] = m_new

    def below_diagonal(ki, carry):
        update(ki, False)
        return carry

    lax.fori_loop(0, qi, below_diagonal, 0)
    update(qi, True)
    for hh in range(hpf):
        lanes = slice(hh * HEAD_W, (hh + 1) * HEAD_W)
        on = acc_scr[hh, :, 0:HEAD_W] / acc_scr[hh, :, HEAD_W:2 * HEAD_W]
        o = on[0:tq] - lam_ref[...] * on[tq:2 * tq]
        o_ref[0, :, lanes] = _subln_gate(o, sub_ref[...], coef,
                                         az_ref[0, :, lanes].astype(f32)).astype(o_ref.dtype)


def _flash_prompt(qz, k_bf, proj, lam_row, subln, coef, tq, hpf=FLASH_HEADS_PER_STEP):
    b, t, _ = k_bf.shape
    n = t // tq
    v_col = (COL_ATT + 2) * N_HEADS // hpf
    az_col = (COL_ATT + 3) * N_HEADS // hpf
    w = hpf * HEAD_W
    return pl.pallas_call(
        functools.partial(_flash_kernel, tq=tq, coef=coef, hpf=hpf),
        grid=(b, N_HEADS // hpf, n),
        in_specs=[pl.BlockSpec((1, 2, tq, w), lambda i, h, q: (i, 0, q, h)),
                  pl.BlockSpec((1, t, w), lambda i, h, q: (i, 0, h)),
                  pl.BlockSpec((1, t, w), lambda i, h, q: (i, 0, v_col + h)),
                  pl.BlockSpec((1, tq, w), lambda i, h, q: (i, q, az_col + h)),
                  pl.BlockSpec((1, HEAD_W), lambda i, h, q: (0, 0)),
                  pl.BlockSpec((1, HEAD_W), lambda i, h, q: (0, 0))],
        out_specs=pl.BlockSpec((1, tq, w), lambda i, h, q: (i, q, h)),
        out_shape=jax.ShapeDtypeStruct((b, t, BRANCH_W), bf16),
        scratch_shapes=[pltpu.VMEM((hpf, 2 * tq, HEAD_W), f32),
                        pltpu.VMEM((hpf, 2 * tq, 2 * HEAD_W), f32)],
        compiler_params=_params(("parallel", "parallel", "parallel")),
    )(qz, k_bf, proj, proj, lam_row, subln.reshape(1, HEAD_W))


def _lane_to_sublane(row):
    r = lax.broadcasted_iota(jnp.int32, (HEAD_W, HEAD_W), 0)
    c = lax.broadcasted_iota(jnp.int32, (HEAD_W, HEAD_W), 1)
    return jnp.sum(jnp.where(r == c, jnp.broadcast_to(row, (HEAD_W, HEAD_W)), 0.0), axis=1, keepdims=True)


def _sample_mix_kernel(p_ref, cbuf_ref, st_ref, cw_ref, lb_ref, onw_ref, qn_ref, kn_ref,
                       cos_ref, sa_ref, sb_ref,
                       ya_ref, yb_ref, nc_ref, nst_ref, kf_ref, qm_ref):
    blk = lambda c: p_ref[0, c * N_HEADS:(c + 1) * N_HEADS, :]
    u = blk(COL_CONV + 2) * blk(COL_CONV)
    y = cw_ref[0] * cbuf_ref[0, 0] + cw_ref[1] * cbuf_ref[0, 1] + cw_ref[2] * u
    ya_ref[0] = (blk(COL_CONV + 1) * y * _silu(blk(COL_CONV + 3))).astype(ya_ref.dtype)
    nc_ref[0, 0] = cbuf_ref[0, 1]
    nc_ref[0, 1] = u
    q, k, g = _rec_gates(blk(COL_REC), blk(COL_REC + 1), lb_ref[...])
    v = blk(COL_REC + 2)
    dec = jnp.exp2(g)
    outs = []
    for h in range(N_HEADS):
        st_new = st_ref[0, h] * dec[h:h + 1, :] + _lane_to_sublane(v[h:h + 1, :]) * k[h:h + 1, :]
        nst_ref[0, h] = st_new
        qh = jnp.broadcast_to(q[h:h + 1, :], (8, HEAD_W)).astype(bf16)
        outs.append(_dot_nt(qh, st_new.astype(bf16))[0:1, :])
    o = jnp.concatenate(outs, axis=0)
    ms = jnp.mean(o * o, axis=-1, keepdims=True)
    yb_ref[0] = (o * lax.rsqrt(ms + EPS) * onw_ref[...] * _silu(blk(COL_REC + 3))).astype(yb_ref.dtype)
    grp = _group_matrix()
    cos_t, sa, sb = cos_ref[...], sa_ref[...], sb_ref[...]
    kf_ref[0] = _qk_norm_rope(blk(COL_ATT + 1), kn_ref[...], grp, cos_t, sa, sb)
    qq = _qk_norm_rope(blk(COL_ATT), qn_ref[...], grp, cos_t, sa, sb) * Q_SCALE
    r = lax.broadcasted_iota(jnp.int32, (2 * N_HEADS, HEAD_W), 0)
    lane = lax.broadcasted_iota(jnp.int32, (2 * N_HEADS, HEAD_W), 1)
    for h in range(N_HEADS):
        qh = jnp.broadcast_to(qq[h:h + 1, :], (2 * N_HEADS, HEAD_W))
        keep = r == 2 * h + jnp.where(lane < ATT_HD, 0, 1)
        qm_ref[0, :, h * HEAD_W:(h + 1) * HEAD_W] = jnp.where(keep, qh, 0.0).astype(bf16)


def _sample_mix(proj_s, conv_state, st_t, conv_w, lb, out_norm, q_norm, k_norm, tables):
    nb = proj_s.shape[0]
    rows = N_IN // HEAD_W
    tile2 = lambda w: jnp.concatenate([w, w]).reshape(1, HEAD_W)
    vec = pl.BlockSpec((1, HEAD_W), lambda i: (0, 0))
    hw = (N_HEADS, HEAD_W)
    out_row = lambda dt: (pl.BlockSpec((1,) + hw, lambda i: (i, 0, 0)), jax.ShapeDtypeStruct((nb,) + hw, dt))
    outs = [out_row(f32), out_row(f32),
            (pl.BlockSpec((1, CONV_WIDTH - 1) + hw, lambda i: (i, 0, 0, 0)),
             jax.ShapeDtypeStruct((nb, CONV_WIDTH - 1) + hw, f32)),
            (pl.BlockSpec((1, N_HEADS, HEAD_W, HEAD_W), lambda i: (i, 0, 0, 0)),
             jax.ShapeDtypeStruct((nb, N_HEADS, HEAD_W, HEAD_W), f32)),
            out_row(f32),
            (pl.BlockSpec((1, 2 * N_HEADS, BRANCH_W), lambda i: (i, 0, 0)),
             jax.ShapeDtypeStruct((nb, 2 * N_HEADS, BRANCH_W), bf16))]
    return pl.pallas_call(
        _sample_mix_kernel,
        grid=(nb,),
        in_specs=[pl.BlockSpec((1, rows, HEAD_W), lambda i: (i, 0, 0)),
                  pl.BlockSpec((1, CONV_WIDTH - 1) + hw, lambda i: (i, 0, 0, 0)),
                  pl.BlockSpec((1, N_HEADS, HEAD_W, HEAD_W), lambda i: (i, 0, 0, 0)),
                  pl.BlockSpec((CONV_WIDTH,) + hw, lambda i: (0, 0, 0)),
                  pl.BlockSpec(hw, lambda i: (0, 0)),
                  vec, vec, vec, vec, vec, vec],
        out_specs=tuple(o[0] for o in outs),
        out_shape=tuple(o[1] for o in outs),
        compiler_params=_params(("parallel",)),
    )(proj_s.reshape(nb, rows, HEAD_W), conv_state.reshape((nb, CONV_WIDTH - 1) + hw), st_t,
      conv_w.reshape((CONV_WIDTH,) + hw), lb.reshape(hw), out_norm.reshape(1, HEAD_W),
      tile2(q_norm), tile2(k_norm), *tables)


def _decode_kernel(pt_ref, qm_ref, kn_ref, vn_ref, az_ref, lam_ref, sub_ref, *rest, coef):
    g = PAGES_PER_STEP
    k_refs, v_refs = rest[:g], rest[g:2 * g]
    o_ref, m_scr, l_scr, acc_scr, fin_scr, exp_scr = rest[2 * g:]
    step = pl.program_id(1)
    q = qm_ref[0]
    n_rows = PAGE_SIZE * N_HEADS
    head_shift = int(math.log2(N_HEADS))

    @pl.when(step == 0)
    def _():
        s0 = jnp.sum(q.astype(f32) * kn_ref[0], axis=-1, keepdims=True)
        m_scr[...] = jnp.broadcast_to(s0, m_scr.shape)
        l_scr[...] = jnp.ones_like(l_scr)
        acc_scr[...] = vn_ref[0]
        pos = lax.broadcasted_iota(jnp.int32, (PAGE_SIZE, n_rows), 0)
        col = lax.broadcasted_iota(jnp.int32, (PAGE_SIZE, n_rows), 1)
        exp_scr[...] = jnp.where((col >> head_shift) == pos, 1.0, 0.0).astype(bf16)

    s = jnp.concatenate([_dot(q, k_refs[i][0, 0].astype(bf16)) for i in range(g)], axis=-1)
    m_old = m_scr[...]
    m_new = jnp.maximum(m_old, jnp.max(s, axis=-1, keepdims=True))
    alpha = jnp.exp2(m_old - m_new)
    p = jnp.exp2(s - jnp.concatenate([m_new] * g, axis=1))
    l_scr[...] = alpha * l_scr[...] + jnp.sum(p, axis=-1, keepdims=True)
    n_q = 2 * N_HEADS
    p_bf = p.astype(bf16)
    stacked = jnp.concatenate([p_bf[:, i * PAGE_SIZE:(i + 1) * PAGE_SIZE] for i in range(g)], axis=0)
    r = lax.broadcasted_iota(jnp.int32, (g * n_q, n_rows), 0)
    col = lax.broadcasted_iota(jnp.int32, (g * n_q, n_rows), 1)
    own_head = (col & (N_HEADS - 1)) == ((r & (n_q - 1)) >> 1)
    spread = jnp.where(own_head, _dot(stacked, exp_scr[...]), 0.0).astype(bf16)
    pv = None
    for i in range(g):
        t = _dot(spread[i * n_q:(i + 1) * n_q], v_refs[i][0, 0].astype(bf16))
        pv = t if pv is None else pv + t
    acc_scr[...] = alpha * acc_scr[...] + pv
    m_scr[...] = m_new

    @pl.when(step == pl.num_programs(1) - 1)
    def _():
        row = lax.broadcasted_iota(jnp.int32, acc_scr.shape, 0)
        fin_scr[...] = acc_scr[...] / l_scr[...] * jnp.where((row & 1) == 0, 1.0, -lam_ref[...])
        o = fin_scr[pl.ds(0, N_HEADS, stride=2), :] + fin_scr[pl.ds(1, N_HEADS, stride=2), :]
        o_ref[0] = _subln_gate(o, sub_ref[...], coef, az_ref[0]).astype(o_ref.dtype)


def _decode_attention(page_table, qmat, k_new, v_new, az, lam_row, subln, cache_k, cache_v, layer, coef):
    nb, n_pages = page_table.shape
    g = PAGES_PER_STEP
    per_batch = lambda rows, width: pl.BlockSpec((1, rows, width), lambda b, s, pt: (b, 0, 0))
    vec = pl.BlockSpec((1, HEAD_W), lambda b, s, pt: (0, 0))
    page = lambda rows, width: [
        pl.BlockSpec((1, 1, rows, width),
                     functools.partial(lambda b, s, pt, i: (layer, pt[b, s * g + i], 0, 0), i=i))
        for i in range(g)]
    acc = pltpu.VMEM((2 * N_HEADS, HEAD_W), f32)
    grid_spec = pltpu.PrefetchScalarGridSpec(
        num_scalar_prefetch=1,
        grid=(nb, n_pages // g),
        in_specs=[per_batch(2 * N_HEADS, BRANCH_W), per_batch(1, BRANCH_W), per_batch(2 * N_HEADS, HEAD_W),
                  per_batch(N_HEADS, HEAD_W), vec, vec]
        + page(BRANCH_W, PAGE_SIZE) + page(PAGE_SIZE * N_HEADS, HEAD_W),
        out_specs=per_batch(N_HEADS, HEAD_W),
        scratch_shapes=[acc, acc, acc, acc, pltpu.VMEM((PAGE_SIZE, PAGE_SIZE * N_HEADS), bf16)])
    return pl.pallas_call(
        functools.partial(_decode_kernel, coef=coef),
        grid_spec=grid_spec,
        out_shape=jax.ShapeDtypeStruct((nb, N_HEADS, HEAD_W), f32),
        compiler_params=_params(("parallel", "arbitrary")),
    )(page_table, qmat, k_new, v_new, az, lam_row, subln.reshape(1, HEAD_W),
      *([cache_k] * g), *([cache_v] * g))


def kernel(x_prompt, x_sample, cache_k, cache_v, state_conv, state_rec, page_table, norm_w, w_in, conv_w,
           rec_lb_logits, rec_out_norm, q_norm, k_norm, lambda_qk, attn_subln, w_branch, w_out):
    nbp, seq, _ = x_prompt.shape
    nbs = x_sample.shape[0]
    n_pool = cache_k.shape[1]
    lb_all, lam_all = _layer_params(rec_lb_logits, lambda_qk)
    w_branch_bf, w_out_bf = w_branch.astype(bf16), w_out.astype(bf16)
    ck = jnp.transpose(cache_k, (0, 1, 3, 4, 5, 2)).reshape(DEPTH, n_pool, BRANCH_W, PAGE_SIZE)
    cv = cache_v.reshape(DEPTH, n_pool, PAGE_SIZE * N_HEADS, HEAD_W)
    tab_p = _rope_tables(jnp.arange(seq, dtype=jnp.int32))
    tab_s = _rope_tables(jnp.full((1,), page_table.shape[1] * PAGE_SIZE, jnp.int32))

    mp = nbp * seq
    tm = min(1024, mp)
    tt = min(512, seq)
    y_p = x_prompt.reshape(mp, D_MODEL)
    ms = -(-nbs // BF16_SUBLANES) * BF16_SUBLANES
    pad_rows = lambda a: jnp.pad(a, ((0, ms - nbs), (0, 0)))
    y_s = pad_rows(x_sample.reshape(nbs, D_MODEL))
    h_p, h_s = _rmsnorm_bf16(y_p, norm_w[0], tm), _rmsnorm_bf16(y_s, norm_w[0], ms)
    outs = [[] for _ in range(8)]
    for l in range(DEPTH):
        coef = 1.0 - (0.8 - 0.6 * math.exp(-0.3 * l))
        lam_row = lam_all[l:l + 1]

        next_norm = norm_w[min(l + 1, DEPTH - 1)]
        proj, proj_pad = _in_proj(h_p, h_s, w_in, l, tm, 1024)
        proj3 = proj.reshape(nbp, seq, N_IN)
        ya, c_p = _conv_prompt(proj3, conv_w[l], tt)
        yb, r_p = _hgrn_prompt(proj3, lb_all[l], rec_out_norm[l], tt)
        k_p, k_bf, qz = _qk_prep(proj3, q_norm[l], k_norm[l], tab_p, min(256, seq))
        yc = _flash_prompt(qz, k_bf, proj3, lam_row, attn_subln[l], coef, tt)
        mm = _merge(ya.reshape(mp, BRANCH_W), yb.reshape(mp, BRANCH_W), yc.reshape(mp, BRANCH_W),
                    w_branch_bf, l, proj, min(256, mp), D_MODEL)
        y_p, h_p = _out_proj(mm, w_out_bf, l, y_p, next_norm, min(256, mp))
        v_p = proj3[:, :, (COL_ATT + 2) * BRANCH_W:(COL_ATT + 3) * BRANCH_W].astype(f32)

        proj_s = proj_pad[:nbs]
        ya_s, yb_s, c_s, st_s, k_s, qmat = _sample_mix(
            proj_s, state_conv[l], jnp.swapaxes(state_rec[l], -1, -2), conv_w[l], lb_all[l],
            rec_out_norm[l], q_norm[l], k_norm[l], tab_s)
        v_s = proj_s[:, (COL_ATT + 2) * BRANCH_W:(COL_ATT + 3) * BRANCH_W]
        az_s = proj_s[:, (COL_ATT + 3) * BRANCH_W:(COL_ATT + 4) * BRANCH_W]
        v_rows = jnp.repeat(v_s.reshape(nbs, N_HEADS, HEAD_W), 2, axis=1)
        yc_s = _decode_attention(page_table, qmat, k_s.reshape(nbs, 1, BRANCH_W), v_rows,
                                 az_s.reshape(nbs, N_HEADS, HEAD_W), lam_row, attn_subln[l], ck, cv, l, coef)
        mm_s = _merge(pad_rows(ya_s.reshape(nbs, BRANCH_W)).astype(bf16),
                      pad_rows(yb_s.reshape(nbs, BRANCH_W)).astype(bf16),
                      pad_rows(yc_s.reshape(nbs, BRANCH_W)).astype(bf16),
                      w_branch_bf, l, proj_pad, ms, D_MODEL)
        y_s, h_s = _out_proj(mm_s, w_out_bf, l, y_s, next_norm, ms)

        for lst, val in zip(outs, (
                k_p.reshape(nbp, seq, N_HEADS, 2, ATT_HD), v_p.reshape(nbp, seq, N_HEADS, HEAD_W),
                k_s.reshape(nbs, 1, N_HEADS, 2, ATT_HD), v_s.reshape(nbs, 1, N_HEADS, HEAD_W),
                c_p, c_s.reshape(nbs, CONV_WIDTH - 1, BRANCH_W),
                jnp.swapaxes(r_p, -1, -2), jnp.swapaxes(st_s, -1, -2))):
            lst.append(val)

    return (y_p.reshape(nbp, seq, D_MODEL), y_s[:nbs].reshape(nbs, 1, D_MODEL)) + tuple(jnp.stack(o, axis=0) for o in outs)
```

```python
import functools
import math

import jax
import jax.numpy as jnp
import numpy as np
from jax import lax
from jax.experimental import pallas as pl
from jax.experimental.pallas import tpu as pltpu

D_MODEL = 2048
DEPTH = 4
PAST_LEN = 16384
PAGE_SIZE = 128
BRANCH_W = D_MODEL // 2
CONV_WIDTH = 3
N_HEADS = 8
HEAD_W = BRANCH_W // N_HEADS
ATT_HD = HEAD_W // 2
ROT_DIM = ATT_HD // 4
ROPE_THETA = 500000.0
N_BRANCH = 3
EPS = 1e-6
MASK_VALUE = -1e30
N_IN = 12 * BRANCH_W + N_BRANCH * D_MODEL
COL_CONV, COL_REC, COL_ATT, COL_GATE = 0, 4, 8, 12

V7X_VMEM_BYTES = 64 * 1024 * 1024
BF16_SUBLANES = 16
VMEM_LIMIT = 56 * 1024 * 1024
REC_CHUNK = 128
REC_SUB = 2
REC_HEADS_PER_STEP = 8
REC_UNROLL = 1
LOG2E = math.log2(math.e)
Q_SCALE = ATT_HD ** -0.5 * LOG2E
FLASH_ROWS = 256
FLASH_HEADS_PER_STEP = 8
PAGES_PER_STEP = 8
W_SLABS = 4

f32 = jnp.float32
bf16 = jnp.bfloat16
PROJ_DTYPE = bf16


def _params(sem, vmem=VMEM_LIMIT):
    return pltpu.CompilerParams(dimension_semantics=sem, vmem_limit_bytes=vmem)


def _sigmoid(x):
    return 1.0 / (1.0 + jnp.exp(-x))


def _sigmoid_tanh(x):
    return 0.5 * jnp.tanh(0.5 * x) + 0.5


def _silu(x):
    return x * _sigmoid(x)


def _dot(a, b):
    return jnp.dot(a, b, preferred_element_type=f32)


def _dot_nt(a, b):
    return lax.dot_general(a, b, (((1,), (1,)), ((), ())), preferred_element_type=f32)


def _dot_tn(a, b):
    return lax.dot_general(a, b, (((0,), (0,)), ((), ())), preferred_element_type=f32)


def _param_kernel(logit_ref, lq_ref, lb_ref, lam_ref):
    x = logit_ref[...]
    m = jnp.max(x, axis=0, keepdims=True)
    e = jnp.exp(x - m)
    soft = e / jnp.sum(e, axis=0, keepdims=True)
    run = soft[0:1]
    rows = [run - soft[0:1]]
    for l in range(1, DEPTH):
        run = run + soft[l:l + 1]
        rows.append(run - soft[0:1])
    lb_ref[...] = jnp.concatenate(rows, axis=0)
    lams = []
    for l in range(DEPTH):
        lq = lq_ref[l]
        a = jnp.sum(lq[0:1] * lq[1:2], axis=-1, keepdims=True)
        b = jnp.sum(lq[2:3] * lq[3:4], axis=-1, keepdims=True)
        lam_init = 0.8 - 0.6 * math.exp(-0.3 * l)
        lams.append(jnp.broadcast_to(jnp.exp(a) - jnp.exp(b) + lam_init, (1, HEAD_W)))
    lam_ref[...] = jnp.concatenate(lams, axis=0)


def _layer_params(rec_lb_logits, lambda_qk):
    return pl.pallas_call(
        _param_kernel,
        out_shape=(jax.ShapeDtypeStruct(rec_lb_logits.shape, f32),
                   jax.ShapeDtypeStruct((DEPTH, HEAD_W), f32)),
    )(rec_lb_logits.astype(f32), lambda_qk.astype(f32))


def _rmsnorm_kernel(x_ref, nw_ref, h_ref):
    x = x_ref[...]
    ms = jnp.mean(x * x, axis=-1, keepdims=True)
    h_ref[...] = (x * lax.rsqrt(ms + EPS) * nw_ref[...]).astype(h_ref.dtype)


def _rmsnorm_bf16(x, norm_w, tm):
    m, d = x.shape
    return pl.pallas_call(
        _rmsnorm_kernel,
        grid=(m // tm,),
        in_specs=[pl.BlockSpec((tm, d), lambda i: (i, 0)), pl.BlockSpec((1, d), lambda i: (0, 0))],
        out_specs=pl.BlockSpec((tm, d), lambda i: (i, 0)),
        out_shape=jax.ShapeDtypeStruct((m, d), bf16),
        compiler_params=_params(("parallel",)),
    )(x, norm_w.reshape(1, d))


def _inproj_kernel(hp_ref, hs_ref, *rest):
    w_refs, (op_ref, os_ref, wb_scr) = rest[:W_SLABS], rest[W_SLABS:]
    slab = wb_scr.shape[0] // W_SLABS

    @pl.when(pl.program_id(1) == 0)
    def _():
        for q, w_ref in enumerate(w_refs):
            wb_scr[q * slab:(q + 1) * slab, :] = w_ref[...].astype(bf16)
        os_ref[...] = _dot(hs_ref[...], wb_scr[...])

    op_ref[...] = _dot(hp_ref[...], wb_scr[...]).astype(op_ref.dtype)


def _in_proj(h_p, h_s, w, layer, tm, tn):
    m, d = h_p.shape
    ms = h_s.shape[0]
    n = w.shape[2]
    n_j = n // tn
    w_specs = [pl.BlockSpec((None, d // W_SLABS, tn),
                            functools.partial(lambda j, i, q: (layer, q, jnp.minimum(j + (i > q), n_j - 1)), q=q))
               for q in range(W_SLABS)]
    return pl.pallas_call(
        _inproj_kernel,
        grid=(n_j, m // tm),
        in_specs=[pl.BlockSpec((tm, d), lambda j, i: (i, 0)),
                  pl.BlockSpec((ms, d), lambda j, i: (0, 0))] + w_specs,
        out_specs=(pl.BlockSpec((tm, tn), lambda j, i: (i, j)),
                   pl.BlockSpec((ms, tn), lambda j, i: (0, j))),
        out_shape=(jax.ShapeDtypeStruct((m, n), PROJ_DTYPE), jax.ShapeDtypeStruct((ms, n), f32)),
        scratch_shapes=[pltpu.VMEM((d, tn), bf16)],
        compiler_params=_params(("parallel", "arbitrary")),
    )(h_p, h_s, *([w] * W_SLABS))


def _merge_kernel(ya_ref, yb_ref, yc_ref, wb_ref, ga_ref, gb_ref, gc_ref, o_ref):
    acc = None
    for n, (y_ref, g_ref) in enumerate(((ya_ref, ga_ref), (yb_ref, gb_ref), (yc_ref, gc_ref))):
        t = _sigmoid_tanh(g_ref[...].astype(f32)) * _dot(y_ref[...], wb_ref[n])
        acc = t if acc is None else acc + t
    o_ref[...] = acc.astype(o_ref.dtype)


def _merge(ya, yb, yc, wb_bf, layer, proj, tm, tn):
    m = ya.shape[0]
    gate0 = COL_GATE * BRANCH_W // tn
    per = D_MODEL // tn
    y_spec = pl.BlockSpec((tm, BRANCH_W), lambda i, j: (i, 0))
    g_specs = [pl.BlockSpec((tm, tn), functools.partial(lambda i, j, n: (i, gate0 + n * per + j), n=n))
               for n in range(N_BRANCH)]
    return pl.pallas_call(
        _merge_kernel,
        grid=(m // tm, D_MODEL // tn),
        in_specs=[y_spec, y_spec, y_spec,
                  pl.BlockSpec((None, N_BRANCH, BRANCH_W, tn), lambda i, j: (layer, 0, 0, j))] + g_specs,
        out_specs=pl.BlockSpec((tm, tn), lambda i, j: (i, j)),
        out_shape=jax.ShapeDtypeStruct((m, D_MODEL), bf16),
        compiler_params=_params(("parallel", "parallel")),
    )(ya, yb, yc, wb_bf, proj, proj, proj)


def _outproj_kernel(m_ref, w_ref, x_ref, nw_ref, o_ref, h_ref):
    y = x_ref[...] + _dot(m_ref[...], w_ref[...])
    o_ref[...] = y
    ms = jnp.mean(y * y, axis=-1, keepdims=True)
    h_ref[...] = (y * lax.rsqrt(ms + EPS) * nw_ref[...]).astype(h_ref.dtype)


def _out_proj(mm, w_bf, layer, x, next_norm_w, tm):
    m = x.shape[0]
    row = pl.BlockSpec((tm, D_MODEL), lambda i: (i, 0))
    return pl.pallas_call(
        _outproj_kernel,
        grid=(m // tm,),
        in_specs=[row,
                  pl.BlockSpec((None, D_MODEL, D_MODEL), lambda i: (layer, 0, 0)),
                  row,
                  pl.BlockSpec((1, D_MODEL), lambda i: (0, 0))],
        out_specs=(row, row),
        out_shape=(jax.ShapeDtypeStruct((m, D_MODEL), f32), jax.ShapeDtypeStruct((m, D_MODEL), bf16)),
        compiler_params=_params(("parallel",)),
    )(mm, w_bf, x, next_norm_w.reshape(1, D_MODEL))


def _conv_kernel(ch_ref, cb_ref, cc_ref, cz_ref, w_ref, ya_ref, nc_ref, carry_ref, *, tt):
    @pl.when(pl.program_id(1) == 0)
    def _():
        carry_ref[...] = jnp.zeros_like(carry_ref)

    u = cc_ref[0].astype(f32) * ch_ref[0].astype(f32)
    prev2 = carry_ref[0:1, :]
    prev1 = carry_ref[1:2, :]
    row = lax.broadcasted_iota(jnp.int32, u.shape, 0)
    u1 = jnp.where(row == 0, prev1, pltpu.roll(u, 1, axis=0))
    u2 = jnp.where(row == 0, prev2, jnp.where(row == 1, prev1, pltpu.roll(u, 2, axis=0)))
    y = w_ref[0:1, :] * u2 + w_ref[1:2, :] * u1 + w_ref[2:3, :] * u
    ya_ref[0] = (cb_ref[0].astype(f32) * y * _silu(cz_ref[0].astype(f32))).astype(ya_ref.dtype)
    tail = u[tt - 2:tt, :]
    carry_ref[0:2, :] = tail
    nc_ref[0] = tail


def _conv_prompt(proj, conv_w, tt):
    b, t, _ = proj.shape
    specs = [pl.BlockSpec((1, tt, BRANCH_W), functools.partial(lambda i, j, c: (i, j, c), c=COL_CONV + c))
             for c in range(4)]
    return pl.pallas_call(
        functools.partial(_conv_kernel, tt=tt),
        grid=(b, t // tt),
        in_specs=specs + [pl.BlockSpec((CONV_WIDTH, BRANCH_W), lambda i, j: (0, 0))],
        out_specs=(pl.BlockSpec((1, tt, BRANCH_W), lambda i, j: (i, j, 0)),
                   pl.BlockSpec((1, CONV_WIDTH - 1, BRANCH_W), lambda i, j: (i, 0, 0))),
        out_shape=(jax.ShapeDtypeStruct((b, t, BRANCH_W), bf16),
                   jax.ShapeDtypeStruct((b, CONV_WIDTH - 1, BRANCH_W), f32)),
        scratch_shapes=[pltpu.VMEM((8, BRANCH_W), f32)],
        compiler_params=_params(("parallel", "arbitrary")),
    )(proj, proj, proj, proj, conv_w)


def _rec_gates(rq, rf, lb):
    e = jnp.exp(-jnp.abs(rf))
    r = 1.0 / (1.0 + e)
    er = e * r
    pos = rf >= 0
    sig = jnp.where(pos, r, er)
    nsig = jnp.where(pos, er, r)
    g = jnp.log(lb + (1.0 - lb) * sig) * LOG2E
    k = (1.0 - lb) * nsig
    return _silu(rq), k, g


def _hgrn_kernel(rq_ref, rf_ref, ri_ref, rg_ref, lb_ref, onw_ref, yb_ref, st_ref, st_scr, *, tt, hp, unroll):
    c_len = REC_CHUNK

    @pl.when(pl.program_id(2) == 0)
    def _():
        st_scr[...] = jnp.zeros_like(st_scr)

    onw = onw_ref[...]
    ri = lax.broadcasted_iota(jnp.int32, (c_len, c_len), 0)
    ci = lax.broadcasted_iota(jnp.int32, (c_len, c_len), 1)
    tri = (ci <= ri).astype(bf16)
    ones_kc = jnp.ones((HEAD_W, c_len), bf16)
    rowk = lax.broadcasted_iota(jnp.int32, (c_len, HEAD_W), 0)
    level_masks = []
    s = REC_SUB
    while s < c_len:
        shift = int(math.log2(2 * s))
        level_masks.append((s, ((((ri ^ ci) >> shift) | ((ri & s) ^ s) | (ci & s)) == 0)))
        s *= 2
    sub_shift = int(math.log2(REC_SUB))
    diag_masks = [((((ri - ci) ^ d) | ((ri >> sub_shift) ^ (ci >> sub_shift))) == 0) for d in range(REC_SUB)]
    valid_rows = [(rowk & (REC_SUB - 1)) >= d for d in range(REC_SUB)]

    def head_chunk(r0, hh):
        rows = pl.ds(r0, c_len)
        lanes = slice(hh * HEAD_W, (hh + 1) * HEAD_W)
        q, k, g = _rec_gates(rq_ref[0, rows, lanes].astype(f32), rf_ref[0, rows, lanes].astype(f32),
                             lb_ref[:, lanes])
        v_bf = ri_ref[0, rows, lanes].astype(bf16)
        g_hi = g.astype(bf16)
        rem = g - g_hi.astype(f32)
        g_mid = rem.astype(bf16)
        g_lo = (rem - g_mid.astype(f32)).astype(bf16)
        b = _dot(tri, g_lo) + _dot(tri, g_mid) + _dot(tri, g_hi)
        b_last = b[c_len - 1:c_len, :]

        q_bf, k_bf = q.astype(bf16), k.astype(bf16)
        a = jnp.zeros((c_len, c_len), f32)
        for s, mask in level_masks:
            refs = [jnp.broadcast_to(b[m + s - 1:m + s, :], (2 * s, HEAD_W)) for m in range(0, c_len, 2 * s)]
            z = jnp.exp2(-jnp.abs(b - jnp.concatenate(refs, axis=0))).astype(bf16)
            a = jnp.where(mask, _dot_nt(q_bf * z, k_bf * z), a)
        for d in range(REC_SUB):
            if d == 0:
                p = q * k
            else:
                ex = jnp.where(valid_rows[d], b - pltpu.roll(b, d, axis=0), 0.0)
                p = q * pltpu.roll(k, d, axis=0) * jnp.exp2(ex)
            a_d = _dot(p.astype(bf16), ones_kc)
            a = jnp.where(diag_masks[d], a_d, a)

        st = st_scr[hh]
        o = _dot(a.astype(bf16), v_bf) + _dot_nt(q_bf * jnp.exp2(b).astype(bf16), st.astype(bf16))
        ms = jnp.mean(o * o, axis=-1, keepdims=True)
        y = o * lax.rsqrt(ms + EPS) * onw * _silu(rg_ref[0, rows, lanes].astype(f32))
        yb_ref[0, rows, lanes] = y.astype(yb_ref.dtype)
        k_dec = k_bf * jnp.exp2(b_last - b).astype(bf16)
        st_scr[hh] = st * jnp.exp2(b_last) + _dot_tn(v_bf, k_dec)

    def chunk(c, carry):
        r0 = pl.multiple_of(c * c_len, c_len)
        for hh in range(hp):
            head_chunk(r0, hh)
        return carry

    lax.fori_loop(0, tt // c_len, chunk, 0, unroll=unroll)

    @pl.when(pl.program_id(2) == pl.num_programs(2) - 1)
    def _():
        st_ref[0] = st_scr[...]


def _hgrn_prompt(proj, lb, out_norm, tt, hp=REC_HEADS_PER_STEP, unroll=REC_UNROLL):
    b, t, _ = proj.shape
    base = COL_REC * N_HEADS // hp
    specs = [pl.BlockSpec((1, tt, hp * HEAD_W),
                          functools.partial(lambda i, h, j, c: (i, j, c + h), c=base + c * N_HEADS // hp))
             for c in range(4)]
    return pl.pallas_call(
        functools.partial(_hgrn_kernel, tt=tt, hp=hp, unroll=unroll),
        grid=(b, N_HEADS // hp, t // tt),
        in_specs=specs + [pl.BlockSpec((1, hp * HEAD_W), lambda i, h, j: (0, h)),
                          pl.BlockSpec((1, HEAD_W), lambda i, h, j: (0, 0))],
        out_specs=(pl.BlockSpec((1, tt, hp * HEAD_W), lambda i, h, j: (i, j, h)),
                   pl.BlockSpec((1, hp, HEAD_W, HEAD_W), lambda i, h, j: (i, h, 0, 0))),
        out_shape=(jax.ShapeDtypeStruct((b, t, BRANCH_W), bf16),
                   jax.ShapeDtypeStruct((b, N_HEADS, HEAD_W, HEAD_W), f32)),
        scratch_shapes=[pltpu.VMEM((hp, HEAD_W, HEAD_W), f32)],
        compiler_params=_params(("parallel", "parallel", "arbitrary")),
    )(proj, proj, proj, proj, lb.reshape(1, BRANCH_W), out_norm.reshape(1, HEAD_W))


def _rope_tables(pos):
    half = ROT_DIM // 2
    inv_freq = ROPE_THETA ** (-jnp.arange(half, dtype=f32) * 2.0 / ROT_DIM)
    ang = pos.astype(f32)[:, None] * inv_freq[None, :]
    cos, sin = jnp.cos(ang), jnp.sin(ang)
    t = pos.shape[0]
    one = jnp.ones((t, ATT_HD - ROT_DIM), f32)
    zero = jnp.zeros((t, ATT_HD - ROT_DIM), f32)
    zh = jnp.zeros((t, half), f32)
    cos_t = jnp.concatenate([cos, cos, one], axis=-1)
    sa = jnp.concatenate([-sin, zh, zero], axis=-1)
    sb = jnp.concatenate([zh, sin, zero], axis=-1)
    return tuple(jnp.concatenate([x, x], axis=-1) for x in (cos_t, sa, sb))


def _component_mean_sq(x, grp):
    sq = x * x
    hi = sq.astype(bf16)
    lo = (sq - hi.astype(f32)).astype(bf16)
    return (_dot(lo, grp) + _dot(hi, grp)) * (1.0 / ATT_HD)


def _qk_norm_rope(x, w, grp, cos_t, sa, sb):
    xn = x * lax.rsqrt(_component_mean_sq(x, grp) + EPS) * w
    half = ROT_DIM // 2
    return xn * cos_t + pltpu.roll(xn, HEAD_W - half, axis=1) * sa + pltpu.roll(xn, half, axis=1) * sb


def _group_matrix():
    r = lax.broadcasted_iota(jnp.int32, (HEAD_W, HEAD_W), 0)
    c = lax.broadcasted_iota(jnp.int32, (HEAD_W, HEAD_W), 1)
    return ((r < ATT_HD) == (c < ATT_HD)).astype(bf16)


def _qkprep_kernel(aq_ref, ak_ref, qn_ref, kn_ref, cos_ref, sa_ref, sb_ref, kf_ref, kb_ref, qz_ref):
    grp = _group_matrix()
    cos_t, sa, sb = cos_ref[...], sa_ref[...], sb_ref[...]
    lane = lax.broadcasted_iota(jnp.int32, cos_t.shape, 1)
    first = lane < ATT_HD
    scale = Q_SCALE
    for h in range(N_HEADS):
        sl = slice(h * HEAD_W, (h + 1) * HEAD_W)
        kk = _qk_norm_rope(ak_ref[0, :, sl].astype(f32), kn_ref[...], grp, cos_t, sa, sb)
        kf_ref[0, :, sl] = kk
        kb_ref[0, :, sl] = kk.astype(bf16)
        qq = _qk_norm_rope(aq_ref[0, :, sl].astype(f32), qn_ref[...], grp, cos_t, sa, sb) * scale
        qz_ref[0, 0, :, sl] = jnp.where(first, qq, 0.0).astype(bf16)
        qz_ref[0, 1, :, sl] = jnp.where(first, 0.0, qq).astype(bf16)


def _qk_prep(proj, q_norm, k_norm, tables, tt):
    b, t, _ = proj.shape
    specs = [pl.BlockSpec((1, tt, BRANCH_W), functools.partial(lambda i, j, c: (i, j, c), c=COL_ATT + c))
             for c in range(2)]
    w_spec = pl.BlockSpec((1, HEAD_W), lambda i, j: (0, 0))
    t_spec = pl.BlockSpec((tt, HEAD_W), lambda i, j: (j, 0))
    o_spec = pl.BlockSpec((1, tt, BRANCH_W), lambda i, j: (i, j, 0))
    tile2 = lambda w: jnp.concatenate([w, w]).reshape(1, HEAD_W)
    return pl.pallas_call(
        _qkprep_kernel,
        grid=(b, t // tt),
        in_specs=specs + [w_spec, w_spec, t_spec, t_spec, t_spec],
        out_specs=(o_spec, o_spec,
                   pl.BlockSpec((1, 2, tt, BRANCH_W), lambda i, j: (i, 0, j, 0))),
        out_shape=(jax.ShapeDtypeStruct((b, t, BRANCH_W), f32),
                   jax.ShapeDtypeStruct((b, t, BRANCH_W), bf16),
                   jax.ShapeDtypeStruct((b, 2, t, BRANCH_W), bf16)),
        compiler_params=_params(("parallel", "parallel")),
    )(proj, proj, tile2(q_norm), tile2(k_norm), *tables)


def _subln_gate(o, subln, coef, az):
    ms = jnp.mean(o * o, axis=-1, keepdims=True)
    return o * lax.rsqrt(ms + EPS) * subln * coef * _silu(az)


def _flash_kernel(qz_ref, k_ref, v_ref, az_ref, lam_ref, sub_ref, o_ref, m_scr, acc_scr,
                  *, tq, coef, hpf):
    rows = min(FLASH_ROWS, tq)
    qi = pl.program_id(2)
    m_scr[...] = jnp.full_like(m_scr, MASK_VALUE)
    acc_scr[...] = jnp.zeros_like(acc_scr)

    def update(ki, on_diagonal):
        keys = pl.ds(pl.multiple_of(ki * tq, tq), tq)
        for hh in range(hpf):
            lanes = slice(hh * HEAD_W, (hh + 1) * HEAD_W)
            k = k_ref[0, keys, lanes]
            v_ones = jnp.concatenate([v_ref[0, keys, lanes], jnp.ones((tq, HEAD_W), bf16)], axis=1)
            for r0 in range(0, 2 * tq, rows):
                comp, q0 = divmod(r0, tq)
                sl = slice(r0, r0 + rows)
                s = _dot_nt(qz_ref[0, comp, q0:q0 + rows, lanes], k)
                if on_diagonal:
                    r = lax.broadcasted_iota(jnp.int32, s.shape, 0)
                    c = lax.broadcasted_iota(jnp.int32, s.shape, 1)
                    s = jnp.where(c <= r + q0, s, MASK_VALUE)
                m_old = m_scr[hh, sl]
                m_new = jnp.maximum(m_old, jnp.max(s, axis=-1, keepdims=True))
                alpha = jnp.exp2(m_old - m_new)
                p = jnp.exp2((s - jnp.concatenate([m_new] * (tq // HEAD_W), axis=1)).astype(bf16))
                acc_scr[hh, sl] = jnp.concatenate([alpha, alpha], axis=1) * acc_scr[hh, sl] + _dot(p, v_ones)
                m_scr[hh, sl] = m_new

    def below_diagonal(ki, carry):
        update(ki, False)
        return carry

    lax.fori_loop(0, qi, below_diagonal, 0)
    update(qi, True)
    for hh in range(hpf):
        lanes = slice(hh * HEAD_W, (hh + 1) * HEAD_W)
        on = acc_scr[hh, :, 0:HEAD_W] / acc_scr[hh, :, HEAD_W:2 * HEAD_W]
        o = on[0:tq] - lam_ref[...] * on[tq:2 * tq]
        o_ref[0, :, lanes] = _subln_gate(o, sub_ref[...], coef,
                                         az_ref[0, :, lanes].astype(f32)).astype(o_ref.dtype)


def _flash_prompt(qz, k_bf, proj, lam_row, subln, coef, tq, hpf=FLASH_HEADS_PER_STEP):
    b, t, _ = k_bf.shape
    n = t // tq
    v_col = (COL_ATT + 2) * N_HEADS // hpf
    az_col = (COL_ATT + 3) * N_HEADS // hpf
    w = hpf * HEAD_W
    return pl.pallas_call(
        functools.partial(_flash_kernel, tq=tq, coef=coef, hpf=hpf),
        grid=(b, N_HEADS // hpf, n),
        in_specs=[pl.BlockSpec((1, 2, tq, w), lambda i, h, q: (i, 0, q, h)),
                  pl.BlockSpec((1, t, w), lambda i, h, q: (i, 0, h)),
                  pl.BlockSpec((1, t, w), lambda i, h, q: (i, 0, v_col + h)),
                  pl.BlockSpec((1, tq, w), lambda i, h, q: (i, q, az_col + h)),
                  pl.BlockSpec((1, HEAD_W), lambda i, h, q: (0, 0)),
                  pl.BlockSpec((1, HEAD_W), lambda i, h, q: (0, 0))],
        out_specs=pl.BlockSpec((1, tq, w), lambda i, h, q: (i, q, h)),
        out_shape=jax.ShapeDtypeStruct((b, t, BRANCH_W), bf16),
        scratch_shapes=[pltpu.VMEM((hpf, 2 * tq, HEAD_W), f32),
                        pltpu.VMEM((hpf, 2 * tq, 2 * HEAD_W), f32)],
        compiler_params=_params(("parallel", "parallel", "parallel")),
    )(qz, k_bf, proj, proj, lam_row, subln.reshape(1, HEAD_W))


def _lane_to_sublane(row):
    r = lax.broadcasted_iota(jnp.int32, (HEAD_W, HEAD_W), 0)
    c = lax.broadcasted_iota(jnp.int32, (HEAD_W, HEAD_W), 1)
    return jnp.sum(jnp.where(r == c, jnp.broadcast_to(row, (HEAD_W, HEAD_W)), 0.0), axis=1, keepdims=True)


def _sample_mix_kernel(p_ref, cbuf_ref, st_ref, cw_ref, lb_ref, onw_ref, qn_ref, kn_ref,
                       cos_ref, sa_ref, sb_ref,
                       ya_ref, yb_ref, nc_ref, nst_ref, kf_ref, qm_ref):
    blk = lambda c: p_ref[0, c * N_HEADS:(c + 1) * N_HEADS, :]
    u = blk(COL_CONV + 2) * blk(COL_CONV)
    y = cw_ref[0] * cbuf_ref[0, 0] + cw_ref[1] * cbuf_ref[0, 1] + cw_ref[2] * u
    ya_ref[0] = (blk(COL_CONV + 1) * y * _silu(blk(COL_CONV + 3))).astype(ya_ref.dtype)
    nc_ref[0, 0] = cbuf_ref[0, 1]
    nc_ref[0, 1] = u
    q, k, g = _rec_gates(blk(COL_REC), blk(COL_REC + 1), lb_ref[...])
    v = blk(COL_REC + 2)
    dec = jnp.exp2(g)
    outs = []
    for h in range(N_HEADS):
        st_new = st_ref[0, h] * dec[h:h + 1, :] + _lane_to_sublane(v[h:h + 1, :]) * k[h:h + 1, :]
        nst_ref[0, h] = st_new
        qh = jnp.broadcast_to(q[h:h + 1, :], (8, HEAD_W)).astype(bf16)
        outs.append(_dot_nt(qh, st_new.astype(bf16))[0:1, :])
    o = jnp.concatenate(outs, axis=0)
    ms = jnp.mean(o * o, axis=-1, keepdims=True)
    yb_ref[0] = (o * lax.rsqrt(ms + EPS) * onw_ref[...] * _silu(blk(COL_REC + 3))).astype(yb_ref.dtype)
    grp = _group_matrix()
    cos_t, sa, sb = cos_ref[...], sa_ref[...], sb_ref[...]
    kf_ref[0] = _qk_norm_rope(blk(COL_ATT + 1), kn_ref[...], grp, cos_t, sa, sb)
    qq = _qk_norm_rope(blk(COL_ATT), qn_ref[...], grp, cos_t, sa, sb) * Q_SCALE
    r = lax.broadcasted_iota(jnp.int32, (2 * N_HEADS, HEAD_W), 0)
    lane = lax.broadcasted_iota(jnp.int32, (2 * N_HEADS, HEAD_W), 1)
    for h in range(N_HEADS):
        qh = jnp.broadcast_to(qq[h:h + 1, :], (2 * N_HEADS, HEAD_W))
        keep = r == 2 * h + jnp.where(lane < ATT_HD, 0, 1)
        qm_ref[0, :, h * HEAD_W:(h + 1) * HEAD_W] = jnp.where(keep, qh, 0.0).astype(bf16)


def _sample_mix(proj_s, conv_state, st_t, conv_w, lb, out_norm, q_norm, k_norm, tables):
    nb = proj_s.shape[0]
    rows = N_IN // HEAD_W
    tile2 = lambda w: jnp.concatenate([w, w]).reshape(1, HEAD_W)
    vec = pl.BlockSpec((1, HEAD_W), lambda i: (0, 0))
    hw = (N_HEADS, HEAD_W)
    out_row = lambda dt: (pl.BlockSpec((1,) + hw, lambda i: (i, 0, 0)), jax.ShapeDtypeStruct((nb,) + hw, dt))
    outs = [out_row(f32), out_row(f32),
            (pl.BlockSpec((1, CONV_WIDTH - 1) + hw, lambda i: (i, 0, 0, 0)),
             jax.ShapeDtypeStruct((nb, CONV_WIDTH - 1) + hw, f32)),
            (pl.BlockSpec((1, N_HEADS, HEAD_W, HEAD_W), lambda i: (i, 0, 0, 0)),
             jax.ShapeDtypeStruct((nb, N_HEADS, HEAD_W, HEAD_W), f32)),
            out_row(f32),
            (pl.BlockSpec((1, 2 * N_HEADS, BRANCH_W), lambda i: (i, 0, 0)),
             jax.ShapeDtypeStruct((nb, 2 * N_HEADS, BRANCH_W), bf16))]
    return pl.pallas_call(
        _sample_mix_kernel,
        grid=(nb,),
        in_specs=[pl.BlockSpec((1, rows, HEAD_W), lambda i: (i, 0, 0)),
                  pl.BlockSpec((1, CONV_WIDTH - 1) + hw, lambda i: (i, 0, 0, 0)),
                  pl.BlockSpec((1, N_HEADS, HEAD_W, HEAD_W), lambda i: (i, 0, 0, 0)),
                  pl.BlockSpec((CONV_WIDTH,) + hw, lambda i: (0, 0, 0)),
                  pl.BlockSpec(hw, lambda i: (0, 0)),
                  vec, vec, vec, vec, vec, vec],
        out_specs=tuple(o[0] for o in outs),
        out_shape=tuple(o[1] for o in outs),
        compiler_params=_params(("parallel",)),
    )(proj_s.reshape(nb, rows, HEAD_W), conv_state.reshape((nb, CONV_WIDTH - 1) + hw), st_t,
      conv_w.reshape((CONV_WIDTH,) + hw), lb.reshape(hw), out_norm.reshape(1, HEAD_W),
      tile2(q_norm), tile2(k_norm), *tables)


def _decode_kernel(pt_ref, qm_ref, kn_ref, vn_ref, az_ref, lam_ref, sub_ref, *rest, coef):
    g = PAGES_PER_STEP
    k_refs, v_refs = rest[:g], rest[g:2 * g]
    o_ref, m_scr, l_scr, acc_scr, fin_scr, exp_scr = rest[2 * g:]
    step = pl.program_id(1)
    q = qm_ref[0]
    n_rows = PAGE_SIZE * N_HEADS
    head_shift = int(math.log2(N_HEADS))

    @pl.when(step == 0)
    def _():
        s0 = jnp.sum(q.astype(f32) * kn_ref[0], axis=-1, keepdims=True)
        m_scr[...] = jnp.broadcast_to(s0, m_scr.shape)
        l_scr[...] = jnp.ones_like(l_scr)
        acc_scr[...] = vn_ref[0]
        pos = lax.broadcasted_iota(jnp.int32, (PAGE_SIZE, n_rows), 0)
        col = lax.broadcasted_iota(jnp.int32, (PAGE_SIZE, n_rows), 1)
        exp_scr[...] = jnp.where((col >> head_shift) == pos, 1.0, 0.0).astype(bf16)

    s = jnp.concatenate([_dot(q, k_refs[i][0, 0].astype(bf16)) for i in range(g)], axis=-1)
    m_old = m_scr[...]
    m_new = jnp.maximum(m_old, jnp.max(s, axis=-1, keepdims=True))
    alpha = jnp.exp2(m_old - m_new)
    p = jnp.exp2(s - jnp.concatenate([m_new] * g, axis=1))
    l_scr[...] = alpha * l_scr[...] + jnp.sum(p, axis=-1, keepdims=True)
    n_q = 2 * N_HEADS
    p_bf = p.astype(bf16)
    stacked = jnp.concatenate([p_bf[:, i * PAGE_SIZE:(i + 1) * PAGE_SIZE] for i in range(g)], axis=0)
    r = lax.broadcasted_iota(jnp.int32, (g * n_q, n_rows), 0)
    col = lax.broadcasted_iota(jnp.int32, (g * n_q, n_rows), 1)
    own_head = (col & (N_HEADS - 1)) == ((r & (n_q - 1)) >> 1)
    spread = jnp.where(own_head, _dot(stacked, exp_scr[...]), 0.0).astype(bf16)
    pv = None
    for i in range(g):
        t = _dot(spread[i * n_q:(i + 1) * n_q], v_refs[i][0, 0].astype(bf16))
        pv = t if pv is None else pv + t
    acc_scr[...] = alpha * acc_scr[...] + pv
    m_scr[...] = m_new

    @pl.when(step == pl.num_programs(1) - 1)
    def _():
        row = lax.broadcasted_iota(jnp.int32, acc_scr.shape, 0)
        fin_scr[...] = acc_scr[...] / l_scr[...] * jnp.where((row & 1) == 0, 1.0, -lam_ref[...])
        o = fin_scr[pl.ds(0, N_HEADS, stride=2), :] + fin_scr[pl.ds(1, N_HEADS, stride=2), :]
        o_ref[0] = _subln_gate(o, sub_ref[...], coef, az_ref[0]).astype(o_ref.dtype)


def _decode_attention(page_table, qmat, k_new, v_new, az, lam_row, subln, cache_k, cache_v, layer, coef):
    nb, n_pages = page_table.shape
    g = PAGES_PER_STEP
    per_batch = lambda rows, width: pl.BlockSpec((1, rows, width), lambda b, s, pt: (b, 0, 0))
    vec = pl.BlockSpec((1, HEAD_W), lambda b, s, pt: (0, 0))
    page = lambda rows, width: [
        pl.BlockSpec((1, 1, rows, width),
                     functools.partial(lambda b, s, pt, i: (layer, pt[b, s * g + i], 0, 0), i=i))
        for i in range(g)]
    acc = pltpu.VMEM((2 * N_HEADS, HEAD_W), f32)
    grid_spec = pltpu.PrefetchScalarGridSpec(
        num_scalar_prefetch=1,
        grid=(nb, n_pages // g),
        in_specs=[per_batch(2 * N_HEADS, BRANCH_W), per_batch(1, BRANCH_W), per_batch(2 * N_HEADS, HEAD_W),
                  per_batch(N_HEADS, HEAD_W), vec, vec]
        + page(BRANCH_W, PAGE_SIZE) + page(PAGE_SIZE * N_HEADS, HEAD_W),
        out_specs=per_batch(N_HEADS, HEAD_W),
        scratch_shapes=[acc, acc, acc, acc, pltpu.VMEM((PAGE_SIZE, PAGE_SIZE * N_HEADS), bf16)])
    return pl.pallas_call(
        functools.partial(_decode_kernel, coef=coef),
        grid_spec=grid_spec,
        out_shape=jax.ShapeDtypeStruct((nb, N_HEADS, HEAD_W), f32),
        compiler_params=_params(("parallel", "arbitrary")),
    )(page_table, qmat, k_new, v_new, az, lam_row, subln.reshape(1, HEAD_W),
      *([cache_k] * g), *([cache_v] * g))


def kernel(x_prompt, x_sample, cache_k, cache_v, state_conv, state_rec, page_table, norm_w, w_in, conv_w,
           rec_lb_logits, rec_out_norm, q_norm, k_norm, lambda_qk, attn_subln, w_branch, w_out):
    nbp, seq, _ = x_prompt.shape
    nbs = x_sample.shape[0]
    n_pool = cache_k.shape[1]
    lb_all, lam_all = _layer_params(rec_lb_logits, lambda_qk)
    w_branch_bf, w_out_bf = w_branch.astype(bf16), w_out.astype(bf16)
    ck = jnp.transpose(cache_k, (0, 1, 3, 4, 5, 2)).reshape(DEPTH, n_pool, BRANCH_W, PAGE_SIZE)
    cv = cache_v.reshape(DEPTH, n_pool, PAGE_SIZE * N_HEADS, HEAD_W)
    tab_p = _rope_tables(jnp.arange(seq, dtype=jnp.int32))
    tab_s = _rope_tables(jnp.full((1,), page_table.shape[1] * PAGE_SIZE, jnp.int32))

    mp = nbp * seq
    tm = min(1024, mp)
    tt = min(512, seq)
    y_p = x_prompt.reshape(mp, D_MODEL)
    ms = -(-nbs // BF16_SUBLANES) * BF16_SUBLANES
    pad_rows = lambda a: jnp.pad(a, ((0, ms - nbs), (0, 0)))
    y_s = pad_rows(x_sample.reshape(nbs, D_MODEL))
    h_p, h_s = _rmsnorm_bf16(y_p, norm_w[0], tm), _rmsnorm_bf16(y_s, norm_w[0], ms)
    outs = [[] for _ in range(8)]
    for l in range(DEPTH):
        coef = 1.0 - (0.8 - 0.6 * math.exp(-0.3 * l))
        lam_row = lam_all[l:l + 1]

        next_norm = norm_w[min(l + 1, DEPTH - 1)]
        proj, proj_pad = _in_proj(h_p, h_s, w_in, l, tm, 1024)
        proj3 = proj.reshape(nbp, seq, N_IN)
        ya, c_p = _conv_prompt(proj3, conv_w[l], tt)
        yb, r_p = _hgrn_prompt(proj3, lb_all[l], rec_out_norm[l], tt)
        k_p, k_bf, qz = _qk_prep(proj3, q_norm[l], k_norm[l], tab_p, min(256, seq))
        yc = _flash_prompt(qz, k_bf, proj3, lam_row, attn_subln[l], coef, tt)
        mm = _merge(ya.reshape(mp, BRANCH_W), yb.reshape(mp, BRANCH_W), yc.reshape(mp, BRANCH_W),
                    w_branch_bf, l, proj, min(256, mp), D_MODEL)
        y_p, h_p = _out_proj(mm, w_out_bf, l, y_p, next_norm, min(256, mp))
        v_p = proj3[:, :, (COL_ATT + 2) * BRANCH_W:(COL_ATT + 3) * BRANCH_W].astype(f32)

        proj_s = proj_pad[:nbs]
        ya_s, yb_s, c_s, st_s, k_s, qmat = _sample_mix(
            proj_s, state_conv[l], jnp.swapaxes(state_rec[l], -1, -2), conv_w[l], lb_all[l],
            rec_out_norm[l], q_norm[l], k_norm[l], tab_s)
        v_s = proj_s[:, (COL_ATT + 2) * BRANCH_W:(COL_ATT + 3) * BRANCH_W]
        az_s = proj_s[:, (COL_ATT + 3) * BRANCH_W:(COL_ATT + 4) * BRANCH_W]
        v_rows = jnp.repeat(v_s.reshape(nbs, N_HEADS, HEAD_W), 2, axis=1)
        yc_s = _decode_attention(page_table, qmat, k_s.reshape(nbs, 1, BRANCH_W), v_rows,
                                 az_s.reshape(nbs, N_HEADS, HEAD_W), lam_row, attn_subln[l], ck, cv, l, coef)
        mm_s = _merge(pad_rows(ya_s.reshape(nbs, BRANCH_W)).astype(bf16),
                      pad_rows(yb_s.reshape(nbs, BRANCH_W)).astype(bf16),
                      pad_rows(yc_s.reshape(nbs, BRANCH_W)).astype(bf16),
                      w_branch_bf, l, proj_pad, ms, D_MODEL)
        y_s, h_s = _out_proj(mm_s, w_out_bf, l, y_s, next_norm, ms)

        for lst, val in zip(outs, (
                k_p.reshape(nbp, seq, N_HEADS, 2, ATT_HD), v_p.reshape(nbp, seq, N_HEADS, HEAD_W),
                k_s.reshape(nbs, 1, N_HEADS, 2, ATT_HD), v_s.reshape(nbs, 1, N_HEADS, HEAD_W),
                c_p, c_s.reshape(nbs, CONV_WIDTH - 1, BRANCH_W),
                jnp.swapaxes(r_p, -1, -2), jnp.swapaxes(st_s, -1, -2))):
            lst.append(val)

    return (y_p.reshape(nbp, seq, D_MODEL), y_s[:nbs].reshape(nbs, 1, D_MODEL)) + tuple(jnp.stack(o, axis=0) for o in outs)
```
